```python
import math
import jax, jax.numpy as jnp
from jax import lax
import numpy as np

D_MODEL = 1024
BATCH = 2
SEQ = 16384
DEPTH = 2

SB_HEADS = 4
SB_HEAD_DIM = 128
SB_WIDTH = SB_HEADS * SB_HEAD_DIM
SB_BLOCK = 128
SSM_D_INNER = 3 * D_MODEL // 2
SSM_HEAD_DIM = 64
SSM_HEADS = SSM_D_INNER // SSM_HEAD_DIM
SSM_GROUPS = 8
SSM_STATE = 128
SSM_CONV = 4
SSM_CHUNK = 128
SSM_CONV_DIM = SSM_D_INNER + 2 * SSM_GROUPS * SSM_STATE
FFN_HIDDEN = 2816
NORM_EPS = 1e-6
DT_MIN = 1e-3
DT_MAX = 1e-1
IN_WIDTH = 3 * SB_WIDTH + SSM_D_INNER + SSM_CONV_DIM + SSM_HEADS + 2 * D_MODEL
IN_SPLITS = (
    SB_WIDTH,
    2 * SB_WIDTH,
    3 * SB_WIDTH,
    3 * SB_WIDTH + SSM_D_INNER,
    3 * SB_WIDTH + SSM_D_INNER + SSM_CONV_DIM,
    3 * SB_WIDTH + SSM_D_INNER + SSM_CONV_DIM + SSM_HEADS,
    3 * SB_WIDTH + SSM_D_INNER + SSM_CONV_DIM + SSM_HEADS + D_MODEL,
)

kernel_name = "hybrid_stickbreaking_mamba2_macaron"


def rms_norm(x, gain):
    x32 = x.astype(jnp.float32)
    y = x32 * lax.rsqrt(jnp.mean(x32 * x32, axis=-1, keepdims=True) + NORM_EPS)
    return (y * gain.astype(jnp.float32)).astype(x.dtype)


def swiglu(h, w_up, w_down):
    gate, up = jnp.split(h @ w_up, 2, axis=-1)
    return (jax.nn.silu(gate) * up) @ w_down


def stick_breaking_attention(q, k, v):
    b, s, h, dh = q.shape
    nb = s // SB_BLOCK
    q32 = q.astype(jnp.float32) * (dh ** -0.5)
    k32 = k.astype(jnp.float32)
    v32 = v.astype(jnp.float32)
    suffix_in_block = jnp.tril(jnp.ones((SB_BLOCK, SB_BLOCK), jnp.float32), k=-1)
    suffix_blocks = jnp.tril(jnp.ones((nb, nb), jnp.float32), k=-1)
    outs = []
    for i in range(nb):
        lo, hi = i * SB_BLOCK, (i + 1) * SB_BLOCK
        q_blk = q32[:, lo:hi]
        k_blk = k32[:, :hi].reshape(b, i + 1, SB_BLOCK, h, dh)
        v_blk = v32[:, :hi].reshape(b, i + 1, SB_BLOCK, h, dh)
        z = jnp.einsum('bqhd,bjkhd->bhqjk', q_blk, k_blk)
        key_pos = jnp.arange(hi).reshape(i + 1, SB_BLOCK)
        q_pos = lo + jnp.arange(SB_BLOCK)
        mask = key_pos[None] < q_pos[:, None, None]
        sp = jax.nn.softplus(z)
        log_keep = jnp.where(mask, -sp, 0.0)
        within = jnp.einsum('bhqjk,kl->bhqjl', log_keep, suffix_in_block)
        cross = jnp.einsum('bhqj,jm->bhqm', log_keep.sum(-1),
                           suffix_blocks[:i + 1, :i + 1])
        w = jnp.where(mask, jnp.exp(z - sp + within + cross[..., None]), 0.0)
        outs.append(jnp.einsum('bhqjk,bjkhd->bqhd', w, v_blk))
    return jnp.concatenate(outs, axis=1).astype(q.dtype)


def causal_depthwise_conv(x, w, bias):
    k = w.shape[0]
    out = lax.conv_general_dilated(
        x, w[:, None, :], window_strides=(1,), padding=[(k - 1, 0)],
        dimension_numbers=('NWC', 'WIO', 'NWC'), feature_group_count=x.shape[-1])
    return out + bias


def ssd_chunked_scan(x, dt, a, bm, cm):
    b, s, h, p = x.shape
    g = SSM_GROUPS
    hg = h // g
    n = bm.shape[-1]
    q = SSM_CHUNK
    nc = s // q

    def chunks(t):
        return jnp.moveaxis(t.reshape((b, nc, q) + t.shape[2:]), 1, 0)

    xc = chunks(x.astype(jnp.float32).reshape(b, s, g, hg, p))
    dtc = chunks(dt.astype(jnp.float32).reshape(b, s, g, hg))
    bc = chunks(bm.astype(jnp.float32))
    cc = chunks(cm.astype(jnp.float32))
    a_g = a.astype(jnp.float32).reshape(g, hg)
    causal = jnp.tril(jnp.ones((q, q), dtype=bool))

    def step(state, inp):
        x_k, dt_k, b_k, c_k = inp
        cum = jnp.cumsum(dt_k * a_g, axis=1)
        cum_t = jnp.moveaxis(cum, 1, -1)
        decay = jnp.exp(jnp.where(causal, cum_t[..., :, None] - cum_t[..., None, :], -jnp.inf))
        xdt = x_k * dt_k[..., None]
        cb = jnp.einsum('bqgn,bkgn->bgqk', c_k, b_k)
        y_intra = jnp.einsum('bgqk,bghqk,bkghp->bqghp', cb, decay, xdt)
        y_inter = jnp.einsum('bqgn,bghpn->bqghp', c_k, state) * jnp.exp(cum)[..., None]
        to_end = jnp.exp(cum[:, -1:] - cum)
        new_state = (state * jnp.exp(cum[:, -1])[..., None, None]
                     + jnp.einsum('bkgn,bkgh,bkghp->bghpn', b_k, to_end, xdt))
        return new_state, y_intra + y_inter

    init = jnp.zeros((b, g, hg, p, n), jnp.float32)
    _, ys = lax.scan(step, init, (xc, dtc, bc, cc))
    return jnp.moveaxis(ys, 0, 1).reshape(b, s, h, p)


def mamba2_mixer(z, xbc, dt_raw, conv_w, conv_b, dt_bias, a_log, d_skip, ssm_norm):
    b, s, _ = xbc.shape
    xbc = jax.nn.silu(causal_depthwise_conv(xbc, conv_w, conv_b))
    xs, bm, cm = jnp.split(xbc, [SSM_D_INNER, SSM_D_INNER + SSM_GROUPS * SSM_STATE], axis=-1)
    xs = xs.reshape(b, s, SSM_HEADS, SSM_HEAD_DIM)
    bm = bm.reshape(b, s, SSM_GROUPS, SSM_STATE)
    cm = cm.reshape(b, s, SSM_GROUPS, SSM_STATE)
    dt = jax.nn.softplus(dt_raw.astype(jnp.float32) + dt_bias.astype(jnp.float32))
    a = -jnp.exp(a_log.astype(jnp.float32))
    y = ssd_chunked_scan(xs, dt, a, bm, cm)
    y = y + d_skip.astype(jnp.float32)[:, None] * xs.astype(jnp.float32)
    y = y.reshape(b, s, SSM_D_INNER) * jax.nn.silu(z.astype(jnp.float32))
    yg = y.reshape(b, s, SSM_GROUPS, SSM_D_INNER // SSM_GROUPS)
    yg = yg * lax.rsqrt(jnp.mean(yg * yg, axis=-1, keepdims=True) + NORM_EPS)
    return (yg.reshape(b, s, SSM_D_INNER) * ssm_norm.astype(jnp.float32)).astype(z.dtype)


def setup_inputs(seed: int = 0) -> dict:
    key = jax.random.key(seed)
    ks = jax.random.split(key, 24)

    def dense(k, shape, fan_in):
        return jax.random.normal(k, shape, jnp.float32) * (fan_in ** -0.5)

    def gain(k, width):
        return 1.0 + 0.02 * jax.random.normal(k, (DEPTH, width), jnp.float32)

    dt = jnp.exp(jax.random.uniform(ks[8], (DEPTH, SSM_HEADS), jnp.float32,
                                    minval=math.log(DT_MIN), maxval=math.log(DT_MAX)))
    dt_bias = dt + jnp.log(-jnp.expm1(-dt))
    a_log = jnp.log(jax.random.uniform(ks[9], (DEPTH, SSM_HEADS), jnp.float32, minval=1.0, maxval=16.0))

    return {
        "x": jax.random.normal(ks[0], (BATCH, SEQ, D_MODEL), jnp.float32),
        "ffn1_norm": gain(ks[1], D_MODEL),
        "ffn1_w_up": dense(ks[2], (DEPTH, D_MODEL, 2 * FFN_HIDDEN), D_MODEL),
        "ffn1_w_down": dense(ks[3], (DEPTH, FFN_HIDDEN, D_MODEL), FFN_HIDDEN),
        "mix_norm": gain(ks[4], D_MODEL),
        "w_in": dense(ks[5], (DEPTH, D_MODEL, IN_WIDTH), D_MODEL),
        "conv_w": dense(ks[6], (DEPTH, SSM_CONV, SSM_CONV_DIM), SSM_CONV),
        "conv_b": 0.02 * jax.random.normal(ks[7], (DEPTH, SSM_CONV_DIM), jnp.float32),
        "dt_bias": dt_bias,
        "a_log": a_log,
        "d_skip": 1.0 + 0.02 * jax.random.normal(ks[10], (DEPTH, SSM_HEADS), jnp.float32),
        "ssm_norm": gain(ks[11], SSM_D_INNER),
        "w_branch_sb": dense(ks[12], (DEPTH, SB_WIDTH, D_MODEL), SB_WIDTH),
        "w_branch_ssm": dense(ks[13], (DEPTH, SSM_D_INNER, D_MODEL), SSM_D_INNER),
        "w_out": dense(ks[14], (DEPTH, D_MODEL, D_MODEL), D_MODEL),
        "ffn2_norm": gain(ks[15], D_MODEL),
        "ffn2_w_up": dense(ks[16], (DEPTH, D_MODEL, 2 * FFN_HIDDEN), D_MODEL),
        "ffn2_w_down": dense(ks[17], (DEPTH, FFN_HIDDEN, D_MODEL), FFN_HIDDEN),
        "final_norm": 1.0 + 0.02 * jax.random.normal(ks[18], (D_MODEL,), jnp.float32),
    }


def reference(x, ffn1_norm, ffn1_w_up, ffn1_w_down, mix_norm, w_in, conv_w, conv_b,
              dt_bias, a_log, d_skip, ssm_norm, w_branch_sb, w_branch_ssm, w_out,
              ffn2_norm, ffn2_w_up, ffn2_w_down, final_norm):
    b, s, _ = x.shape
    for l in range(DEPTH):
        x = x + 0.5 * swiglu(rms_norm(x, ffn1_norm[l]), ffn1_w_up[l], ffn1_w_down[l])
        h = rms_norm(x, mix_norm[l])
        proj = h @ w_in[l]
        q, k, v, z, xbc, dt_raw, gate_sb, gate_ssm = jnp.split(proj, IN_SPLITS, axis=-1)
        heads = (b, s, SB_HEADS, SB_HEAD_DIM)
        y_sb = stick_breaking_attention(q.reshape(heads), k.reshape(heads), v.reshape(heads))
        y_sb = y_sb.reshape(b, s, SB_WIDTH) @ w_branch_sb[l]
        y_ssm = mamba2_mixer(z, xbc, dt_raw, conv_w[l], conv_b[l], dt_bias[l],
                             a_log[l], d_skip[l], ssm_norm[l]) @ w_branch_ssm[l]
        merged = jax.nn.sigmoid(gate_sb) * y_sb + jax.nn.sigmoid(gate_ssm) * y_ssm
        x = x + merged @ w_out[l]
        x = x + 0.5 * swiglu(rms_norm(x, ffn2_norm[l]), ffn2_w_up[l], ffn2_w_down[l])
    return rms_norm(x, final_norm)
```

```python
import functools

import jax
import jax.numpy as jnp
from jax import lax
from jax.experimental import pallas as pl
from jax.experimental.pallas import tpu as pltpu

F32 = jnp.float32
BF16 = jnp.bfloat16

D_MODEL = 1024
SB_HEADS = 4
SB_HEAD_DIM = 128
SB_WIDTH = SB_HEADS * SB_HEAD_DIM
SSM_D_INNER = 1536
SSM_HEAD_DIM = 64
SSM_HEADS = SSM_D_INNER // SSM_HEAD_DIM
SSM_GROUPS = 8
SSM_HEADS_PER_GROUP = SSM_HEADS // SSM_GROUPS
SSM_STATE = 128
SSM_CONV = 4
SSM_CONV_DIM = SSM_D_INNER + 2 * SSM_GROUPS * SSM_STATE
SSM_GROUP_WIDTH = SSM_D_INNER // SSM_GROUPS
FFN_HIDDEN = 2816
NORM_EPS = 1e-6

LANES = 128
SUBLANES = 8
VMEM_LIMIT = 56 * 1024 * 1024

FFN_TOKEN_TILE = 512
FFN_HIDDEN_CHUNK = FFN_HIDDEN // 2
PROJ_TOKEN_TILE = 256
MERGE_TOKEN_TILE = 512
SB_TILE = 128
SSM_CHUNK = 128
CONV_COL_TILE = 512
HALO = SUBLANES


def _params(n_axes):
    return pltpu.CompilerParams(
        dimension_semantics=("arbitrary",) * n_axes, vmem_limit_bytes=VMEM_LIMIT)


def _resident(shape):
    zeros = (0,) * len(shape)
    return pl.BlockSpec(shape, lambda *_: zeros, pipeline_mode=pl.Buffered(1))


def _rms_norm(x, gain):
    ms = jnp.mean(x * x, axis=-1, keepdims=True)
    return x * lax.rsqrt(ms + NORM_EPS) * gain


def _silu(x):
    return x * jax.nn.sigmoid(x)


def _softplus(x):
    return jnp.maximum(x, 0.0) + jnp.log1p(jnp.exp(-jnp.abs(x)))


def _dot(a, b):
    return jnp.dot(a, b, preferred_element_type=F32)


def _split_bf16(x, parts):
    out = []
    for _ in range(parts - 1):
        hi = x.astype(BF16)
        out.append(hi)
        x = x - hi.astype(F32)
    out.append(x.astype(BF16))
    return out


def _ffn_kernel(x_ref, g_ref, wg_ref, wu_ref, wd_ref, *rest, final_norm):
    if final_norm:
        fg_ref, o_ref = rest
    else:
        (o_ref,) = rest
    x = x_ref[...]
    xn = _rms_norm(x, g_ref[...]).astype(BF16)
    acc = None
    for c in range(FFN_HIDDEN // FFN_HIDDEN_CHUNK):
        cols = slice(c * FFN_HIDDEN_CHUNK, (c + 1) * FFN_HIDDEN_CHUNK)
        gate = _dot(xn, wg_ref[:, cols])
        up = _dot(xn, wu_ref[:, cols])
        act = (_silu(gate) * up).astype(BF16)
        part = _dot(act, wd_ref[cols, :])
        acc = part if acc is None else acc + part
    y = x + 0.5 * acc
    if final_norm:
        y = _rms_norm(y, fg_ref[...])
    o_ref[...] = y


def _ffn(x, gain, w_up, w_down, final_gain):
    t, d = x.shape
    tm = FFN_TOKEN_TILE
    w_gate = w_up[:, :FFN_HIDDEN].astype(BF16)
    w_lin = w_up[:, FFN_HIDDEN:].astype(BF16)
    w_dn = w_down.astype(BF16)
    row = pl.BlockSpec((tm, d), lambda i: (i, 0))
    in_specs = [row, _resident((1, d)), _resident((d, FFN_HIDDEN)),
                _resident((d, FFN_HIDDEN)), _resident((FFN_HIDDEN, d))]
    args = [x, gain.reshape(1, d), w_gate, w_lin, w_dn]
    if final_gain is not None:
        in_specs.append(_resident((1, d)))
        args.append(final_gain.reshape(1, d))
    return pl.pallas_call(
        functools.partial(_ffn_kernel, final_norm=final_gain is not None),
        out_shape=jax.ShapeDtypeStruct((t, d), F32),
        grid=(t // tm,),
        in_specs=in_specs,
        out_specs=row,
        compiler_params=_params(1),
        name="ffn",
    )(*args)


def _in_proj_kernel(x_ref, g_ref, wq_ref, wk_ref, wv_ref, wz_ref, wx_ref, wdt_ref,
                    wgs_ref, wgm_ref, q_ref, k_ref, v_ref, z_ref, xbc_ref, dt_ref,
                    gs_ref, gm_ref):
    xn = _rms_norm(x_ref[...], g_ref[...]).astype(BF16)
    q_ref[...] = (_dot(xn, wq_ref[...]) * (SB_HEAD_DIM ** -0.5)).astype(BF16)
    k_ref[...] = _dot(xn, wk_ref[...]).astype(BF16)
    v_ref[...] = _dot(xn, wv_ref[...]).astype(BF16)
    z_ref[...] = _dot(xn, wz_ref[...]).astype(BF16)
    xbc_ref[...] = _dot(xn, wx_ref[...]).astype(BF16)
    dt_ref[...] = _dot(xn, wdt_ref[...])
    gs_ref[...] = _dot(xn, wgs_ref[...]).astype(BF16)
    gm_ref[...] = _dot(xn, wgm_ref[...]).astype(BF16)


def _in_proj(x, gain, w_in):
    t, d = x.shape
    tm = PROJ_TOKEN_TILE
    widths = (SB_WIDTH, SB_WIDTH, SB_WIDTH, SSM_D_INNER, SSM_CONV_DIM, SSM_HEADS,
              D_MODEL, D_MODEL)
    pieces, start = [], 0
    for w in widths:
        pieces.append(w_in[:, start:start + w].astype(BF16))
        start += w
    pieces[5] = jnp.pad(pieces[5], ((0, 0), (0, LANES - SSM_HEADS)))
    out_widths = (SB_WIDTH, SB_WIDTH, SB_WIDTH, SSM_D_INNER, SSM_CONV_DIM, LANES,
                  D_MODEL, D_MODEL)
    out_dtypes = (BF16, BF16, BF16, BF16, BF16, F32, BF16, BF16)
    row = lambda w: pl.BlockSpec((tm, w), lambda i: (i, 0))
    return pl.pallas_call(
        _in_proj_kernel,
        out_shape=[jax.ShapeDtypeStruct((t, w), dt) for w, dt in zip(out_widths, out_dtypes)],
        grid=(t // tm,),
        in_specs=[row(d), _resident((1, d))] + [_resident(p.shape) for p in pieces],
        out_specs=[row(w) for w in out_widths],
        compiler_params=_params(1),
        name="in_proj",
    )(x, gain.reshape(1, d), *pieces)


def _sb_tile(q, kb, vb, sum_mat, total, mask):
    tk = kb.shape[0]
    z = lax.dot_general(q, kb, (((1,), (1,)), ((), ())), preferred_element_type=F32)
    sp = _softplus(z)
    log_keep = -sp if mask is None else jnp.where(mask, -sp, 0.0)
    sums = _dot(jnp.concatenate(_split_bf16(log_keep, 2), axis=1), sum_mat)
    w = jnp.exp(z - sp + sums[:, :tk] + total)
    if mask is not None:
        w = jnp.where(mask, w, 0.0)
    return _dot(w.astype(BF16), vb), sums[:, tk:]


def _sb_kernel(q_ref, k_ref, v_ref, o_ref):
    i = pl.program_id(2)
    t = SB_TILE
    q = q_ref[...]
    rows = lax.broadcasted_iota(jnp.int32, (t, t), 0)
    cols = lax.broadcasted_iota(jnp.int32, (t, t), 1)
    later = (rows > cols).astype(BF16)
    half = jnp.concatenate([later, jnp.ones((t, t), BF16)], axis=1)
    sum_mat = jnp.concatenate([half, half], axis=0)

    def tile(j, total, mask):
        start = pl.multiple_of(j * t, t)
        return _sb_tile(q, k_ref[pl.ds(start, t), :], v_ref[pl.ds(start, t), :],
                        sum_mat, total, mask)

    acc, total = tile(i, jnp.zeros((t, t), F32), cols < rows)

    def body(step, carry):
        acc, total = carry
        part, row_tot = tile(i - 1 - step, total, None)
        return acc + part, total + row_tot

    acc, _ = lax.fori_loop(0, i, body, (acc, total))
    o_ref[...] = acc.astype(o_ref.dtype)


def _sb_attention(q, k, v, batch, seq):
    t = SB_TILE
    nq = seq // t
    qspec = pl.BlockSpec((t, SB_HEAD_DIM), lambda b, h, i: (b * nq + i, h))
    kvspec = pl.BlockSpec((seq, SB_HEAD_DIM), lambda b, h, i: (b, h))
    return pl.pallas_call(
        _sb_kernel,
        out_shape=jax.ShapeDtypeStruct(q.shape, BF16),
        grid=(batch, SB_HEADS, nq),
        in_specs=[qspec, kvspec, kvspec],
        out_specs=qspec,
        compiler_params=_params(3),
        name="sb_attention",
    )(q, k, v)


def _ssd_kernel(z_ref, xbc_ref, dt_ref, convw_ref, convb_ref, dtb_ref, alog_ref,
                dsk_ref, nrm_ref, o_ref, xpad_ref, xc_ref, st_ref, y_ref):
    n = SSM_CHUNK
    c = pl.program_id(1)

    @pl.when(c == 0)
    def _():
        xpad_ref[0:HALO, :] = jnp.zeros((HALO, SSM_CONV_DIM), F32)
        st_ref[...] = jnp.zeros_like(st_ref)

    @pl.when(c != 0)
    def _():
        xpad_ref[0:HALO, :] = xpad_ref[n:n + HALO, :]

    xpad_ref[HALO:HALO + n, :] = xbc_ref[...].astype(F32)

    first = HALO - (SSM_CONV - 1)
    for ct in range(SSM_CONV_DIM // CONV_COL_TILE):
        cs = slice(ct * CONV_COL_TILE, (ct + 1) * CONV_COL_TILE)
        acc = convb_ref[:, cs] + convw_ref[0:1, cs] * xpad_ref[first:first + n, cs]
        for i in range(1, SSM_CONV):
            acc = acc + convw_ref[i:i + 1, cs] * xpad_ref[first + i:first + i + n, cs]
        xc_ref[:, cs] = _silu(acc)

    rows = lax.broadcasted_iota(jnp.int32, (n, n), 0)
    cols = lax.broadcasted_iota(jnp.int32, (n, n), 1)
    causal = rows >= cols
    low = cols < SSM_HEAD_DIM
    low_row = low[0:1, :]

    dt = _softplus(dt_ref[...] + dtb_ref[...])
    d_a = dt * (-jnp.exp(alog_ref[...]))
    incl = causal.astype(BF16)
    cum = _dot(jnp.concatenate([incl] * 3, axis=1),
               jnp.concatenate(_split_bf16(d_a, 3), axis=0))
    cum_t = cum.T
    ecum = jnp.exp(cum)
    dt_t = dt.T
    w_t = (dt * jnp.exp(cum[n - 1:n, :] - cum)).T

    b_off = SSM_D_INNER
    c_off = SSM_D_INNER + SSM_GROUPS * SSM_STATE
    group_cache = {}

    def group(g):
        if g not in group_cache:
            bg = xc_ref[:, b_off + g * SSM_STATE:b_off + (g + 1) * SSM_STATE]
            cg = xc_ref[:, c_off + g * SSM_STATE:c_off + (g + 1) * SSM_STATE]
            cb = lax.dot_general(cg.astype(BF16), bg.astype(BF16),
                                 (((1,), (1,)), ((), ())), preferred_element_type=F32)
            group_cache[g] = (cb, cg, bg.T)
        return group_cache[g]

    for p in range(SSM_HEADS // 2):
        lanes = slice(p * LANES, (p + 1) * LANES)
        xs = xc_ref[:, lanes]
        state = st_ref[p]
        xs_pair = [jnp.where(low, xs, 0.0).astype(BF16), jnp.where(low, 0.0, xs).astype(BF16)]
        st_pair = [jnp.where(low, state, 0.0).astype(BF16),
                   jnp.where(low, 0.0, state).astype(BF16)]
        g_mats, c_mats, b_mats = [], [], []
        for h in (2 * p, 2 * p + 1):
            cb, cg, bg_t = group(h // SSM_HEADS_PER_GROUP)
            decay = jnp.where(causal, jnp.exp(cum[:, h:h + 1] - cum_t[h:h + 1, :]), 0.0)
            g_mats.append((cb * decay * dt_t[h:h + 1, :]).astype(BF16))
            c_mats.append((cg * ecum[:, h:h + 1]).astype(BF16))
            b_mats.append((bg_t * w_t[h:h + 1, :]).astype(BF16))
        y = _dot(jnp.concatenate(g_mats + c_mats, axis=1),
                 jnp.concatenate(xs_pair + st_pair, axis=0))
        y_ref[:, lanes] = y + dsk_ref[:, lanes] * xs
        last = ecum[n - 1:n, :]
        state_decay = jnp.where(low_row, last[:, 2 * p:2 * p + 1], last[:, 2 * p + 1:2 * p + 2])
        st_ref[p] = state * state_decay + _dot(jnp.concatenate(b_mats, axis=1),
                                               jnp.concatenate(xs_pair, axis=0))

    for gp in range(SSM_GROUPS // 2):
        tiles = [slice((3 * gp + i) * LANES, (3 * gp + i + 1) * LANES) for i in range(3)]
        ys = [y_ref[:, s] * _silu(z_ref[:, s].astype(F32)) for s in tiles]
        sq = [v * v for v in ys]
        rsum = lambda v: jnp.sum(v, axis=-1, keepdims=True)
        ss_a = rsum(sq[0]) + rsum(jnp.where(low, sq[1], 0.0))
        ss_b = rsum(sq[2]) + rsum(jnp.where(low, 0.0, sq[1]))
        r_a = lax.rsqrt(ss_a * (1.0 / SSM_GROUP_WIDTH) + NORM_EPS)
        r_b = lax.rsqrt(ss_b * (1.0 / SSM_GROUP_WIDTH) + NORM_EPS)
        scales = [r_a, jnp.where(low, r_a, r_b), r_b]
        for s, v, r in zip(tiles, ys, scales):
            o_ref[:, s] = (v * r * nrm_ref[:, s]).astype(o_ref.dtype)


def _ssd(z, xbc, dt, conv_w, conv_b, dt_bias, a_log, d_skip, ssm_norm, batch, seq):
    n = SSM_CHUNK
    nc = seq // n
    pad = lambda v: jnp.pad(v, (0, LANES - SSM_HEADS)).reshape(1, LANES)
    row = lambda w: pl.BlockSpec((n, w), lambda b, c: (b * nc + c, 0))
    return pl.pallas_call(
        _ssd_kernel,
        out_shape=jax.ShapeDtypeStruct(z.shape, BF16),
        grid=(batch, nc),
        in_specs=[row(SSM_D_INNER), row(SSM_CONV_DIM), row(LANES),
                  _resident((SSM_CONV, SSM_CONV_DIM)), _resident((1, SSM_CONV_DIM)),
                  _resident((1, LANES)), _resident((1, LANES)),
                  _resident((1, SSM_D_INNER)), _resident((1, SSM_D_INNER))],
        out_specs=row(SSM_D_INNER),
        scratch_shapes=[
            pltpu.VMEM((HALO + n, SSM_CONV_DIM), F32),
            pltpu.VMEM((n, SSM_CONV_DIM), F32),
            pltpu.VMEM((SSM_HEADS // 2, SSM_STATE, LANES), F32),
            pltpu.VMEM((n, SSM_D_INNER), F32),
        ],
        compiler_params=_params(2),
        name="ssd",
    )(z, xbc, dt, conv_w, conv_b.reshape(1, -1), pad(dt_bias), pad(a_log),
      jnp.repeat(d_skip, SSM_HEAD_DIM).reshape(1, -1), ssm_norm.reshape(1, -1))


def _merge_kernel(x_ref, ysb_ref, yssm_ref, gs_ref, gm_ref, wsb_ref, wssm_ref, wo_ref, o_ref):
    y_sb = _dot(ysb_ref[...], wsb_ref[...])
    y_ssm = _dot(yssm_ref[...], wssm_ref[...])
    merged = (jax.nn.sigmoid(gs_ref[...].astype(F32)) * y_sb
              + jax.nn.sigmoid(gm_ref[...].astype(F32)) * y_ssm)
    o_ref[...] = x_ref[...] + _dot(merged.astype(BF16), wo_ref[...])


def _merge(x, y_sb, y_ssm, gate_sb, gate_ssm, w_sb, w_ssm, w_out):
    t, d = x.shape
    tm = MERGE_TOKEN_TILE
    row = lambda w: pl.BlockSpec((tm, w), lambda i: (i, 0))
    return pl.pallas_call(
        _merge_kernel,
        out_shape=jax.ShapeDtypeStruct((t, d), F32),
        grid=(t // tm,),
        in_specs=[row(d), row(SB_WIDTH), row(SSM_D_INNER), row(d), row(d),
                  _resident((SB_WIDTH, d)), _resident((SSM_D_INNER, d)), _resident((d, d))],
        out_specs=row(d),
        compiler_params=_params(1),
        name="merge",
    )(x, y_sb, y_ssm, gate_sb, gate_ssm, w_sb.astype(BF16), w_ssm.astype(BF16),
      w_out.astype(BF16))


def kernel(x, ffn1_norm, ffn1_w_up, ffn1_w_down, mix_norm, w_in, conv_w, conv_b, dt_bias,
           a_log, d_skip, ssm_norm, w_branch_sb, w_branch_ssm, w_out, ffn2_norm, ffn2_w_up,
           ffn2_w_down, final_norm):
    batch, seq, d = x.shape
    depth = w_in.shape[0]
    assert d == D_MODEL and seq % SB_TILE == 0 and seq % SSM_CHUNK == 0
    assert (batch * seq) % FFN_TOKEN_TILE == 0
    xf = x.reshape(batch * seq, d)
    for l in range(depth):
        xf = _ffn(xf, ffn1_norm[l], ffn1_w_up[l], ffn1_w_down[l], None)
        q, k, v, z, xbc, dt, gate_sb, gate_ssm = _in_proj(xf, mix_norm[l], w_in[l])
        y_sb = _sb_attention(q, k, v, batch, seq)
        y_ssm = _ssd(z, xbc, dt, conv_w[l], conv_b[l], dt_bias[l], a_log[l], d_skip[l],
                     ssm_norm[l], batch, seq)
        xf = _merge(xf, y_sb, y_ssm, gate_sb, gate_ssm, w_branch_sb[l], w_branch_ssm[l],
                    w_out[l])
        xf = _ffn(xf, ffn2_norm[l], ffn2_w_up[l], ffn2_w_down[l],
                  final_norm if l == depth - 1 else None)
    return xf.reshape(batch, seq, d)
```

```python
import functools

import jax
import jax.numpy as jnp
from jax import lax
from jax.experimental import pallas as pl
from jax.experimental.pallas import tpu as pltpu

F32 = jnp.float32
BF16 = jnp.bfloat16

D_MODEL = 1024
SB_HEADS = 4
SB_HEAD_DIM = 128
SB_WIDTH = SB_HEADS * SB_HEAD_DIM
SSM_D_INNER = 1536
SSM_HEAD_DIM = 64
SSM_HEADS = SSM_D_INNER // SSM_HEAD_DIM
SSM_GROUPS = 8
SSM_HEADS_PER_GROUP = SSM_HEADS // SSM_GROUPS
SSM_STATE = 128
SSM_CONV = 4
SSM_CONV_DIM = SSM_D_INNER + 2 * SSM_GROUPS * SSM_STATE
SSM_GROUP_WIDTH = SSM_D_INNER // SSM_GROUPS
FFN_HIDDEN = 2816
NORM_EPS = 1e-6

LANES = 128
SUBLANES = 8
VMEM_LIMIT = 56 * 1024 * 1024

FFN_TOKEN_TILE = 512
FFN_HIDDEN_CHUNK = FFN_HIDDEN // 2
PROJ_TOKEN_TILE = 256
MERGE_TOKEN_TILE = 512
SB_TILE = 256
SB_SUB = 128
SB_HEADS_PER_STEP = 2
LOG_WEIGHT_FLOOR = -105.0
SSM_CHUNK = 128
CONV_COL_TILE = 512
HALO = SUBLANES


def _params(n_axes):
    return pltpu.CompilerParams(
        dimension_semantics=("arbitrary",) * n_axes, vmem_limit_bytes=VMEM_LIMIT)


def _resident(shape):
    zeros = (0,) * len(shape)
    return pl.BlockSpec(shape, lambda *_: zeros, pipeline_mode=pl.Buffered(1))


def _rms_norm(x, gain):
    ms = jnp.mean(x * x, axis=-1, keepdims=True)
    return x * lax.rsqrt(ms + NORM_EPS) * gain


def _silu(x):
    return x * jax.nn.sigmoid(x)


def _softplus(x):
    return jnp.maximum(x, 0.0) + jnp.log1p(jnp.exp(-jnp.abs(x)))


def _dot(a, b):
    return jnp.dot(a, b, preferred_element_type=F32)


def _split_bf16(x, parts):
    out = []
    for _ in range(parts - 1):
        hi = x.astype(BF16)
        out.append(hi)
        x = x - hi.astype(F32)
    out.append(x.astype(BF16))
    return out


def _ffn_kernel(x_ref, g_ref, wg_ref, wu_ref, wd_ref, *rest, final_norm):
    if final_norm:
        fg_ref, o_ref = rest
    else:
        (o_ref,) = rest
    x = x_ref[...]
    xn = _rms_norm(x, g_ref[...]).astype(BF16)
    acc = None
    for c in range(FFN_HIDDEN // FFN_HIDDEN_CHUNK):
        cols = slice(c * FFN_HIDDEN_CHUNK, (c + 1) * FFN_HIDDEN_CHUNK)
        gate = _dot(xn, wg_ref[:, cols])
        up = _dot(xn, wu_ref[:, cols])
        act = (_silu(gate) * up).astype(BF16)
        part = _dot(act, wd_ref[cols, :])
        acc = part if acc is None else acc + part
    y = x + 0.5 * acc
    if final_norm:
        y = _rms_norm(y, fg_ref[...])
    o_ref[...] = y


def _ffn(x, gain, w_up, w_down, final_gain):
    t, d = x.shape
    tm = FFN_TOKEN_TILE
    w_gate = w_up[:, :FFN_HIDDEN].astype(BF16)
    w_lin = w_up[:, FFN_HIDDEN:].astype(BF16)
    w_dn = w_down.astype(BF16)
    row = pl.BlockSpec((tm, d), lambda i: (i, 0))
    in_specs = [row, _resident((1, d)), _resident((d, FFN_HIDDEN)),
                _resident((d, FFN_HIDDEN)), _resident((FFN_HIDDEN, d))]
    args = [x, gain.reshape(1, d), w_gate, w_lin, w_dn]
    if final_gain is not None:
        in_specs.append(_resident((1, d)))
        args.append(final_gain.reshape(1, d))
    return pl.pallas_call(
        functools.partial(_ffn_kernel, final_norm=final_gain is not None),
        out_shape=jax.ShapeDtypeStruct((t, d), F32),
        grid=(t // tm,),
        in_specs=in_specs,
        out_specs=row,
        compiler_params=_params(1),
        name="ffn",
    )(*args)


def _in_proj_kernel(x_ref, g_ref, wq_ref, wk_ref, wv_ref, wz_ref, wx_ref, wdt_ref,
                    wgs_ref, wgm_ref, q_ref, k_ref, v_ref, z_ref, xbc_ref, dt_ref,
                    gs_ref, gm_ref):
    xn = _rms_norm(x_ref[...], g_ref[...]).astype(BF16)
    q_ref[...] = (_dot(xn, wq_ref[...]) * (SB_HEAD_DIM ** -0.5)).astype(BF16)
    k_ref[...] = _dot(xn, wk_ref[...]).astype(BF16)
    v_ref[...] = _dot(xn, wv_ref[...]).astype(BF16)
    z_ref[...] = _dot(xn, wz_ref[...]).astype(BF16)
    xbc_ref[...] = _dot(xn, wx_ref[...]).astype(BF16)
    dt_ref[...] = _dot(xn, wdt_ref[...])
    gs_ref[...] = _dot(xn, wgs_ref[...]).astype(BF16)
    gm_ref[...] = _dot(xn, wgm_ref[...]).astype(BF16)


def _in_proj(x, gain, w_in):
    t, d = x.shape
    tm = PROJ_TOKEN_TILE
    widths = (SB_WIDTH, SB_WIDTH, SB_WIDTH, SSM_D_INNER, SSM_CONV_DIM, SSM_HEADS,
              D_MODEL, D_MODEL)
    pieces, start = [], 0
    for w in widths:
        pieces.append(w_in[:, start:start + w].astype(BF16))
        start += w
    pieces[5] = jnp.pad(pieces[5], ((0, 0), (0, LANES - SSM_HEADS)))
    out_widths = (SB_WIDTH, SB_WIDTH, SB_WIDTH, SSM_D_INNER, SSM_CONV_DIM, LANES,
                  D_MODEL, D_MODEL)
    out_dtypes = (BF16, BF16, BF16, BF16, BF16, F32, BF16, BF16)
    row = lambda w: pl.BlockSpec((tm, w), lambda i: (i, 0))
    return pl.pallas_call(
        _in_proj_kernel,
        out_shape=[jax.ShapeDtypeStruct((t, w), dt) for w, dt in zip(out_widths, out_dtypes)],
        grid=(t // tm,),
        in_specs=[row(d), _resident((1, d))] + [_resident(p.shape) for p in pieces],
        out_specs=[row(w) for w in out_widths],
        compiler_params=_params(1),
        name="in_proj",
    )(x, gain.reshape(1, d), *pieces)


def _sb_head_tile(q, kb, vb, sum_mat, total, mask):
    t, s = SB_TILE, SB_SUB
    z = lax.dot_general(q, kb, (((1,), (1,)), ((), ())), preferred_element_type=F32)
    sp = _softplus(z)
    log_keep = -sp if mask is None else jnp.where(mask, -sp, 0.0)
    log_beta = z - sp
    parts = [None] * (t // s)
    for b in reversed(range(t // s)):
        cs = slice(b * s, (b + 1) * s)
        sums = _dot(jnp.concatenate(_split_bf16(log_keep[:, cs], 2), axis=1), sum_mat)
        parts[b] = log_beta[:, cs] + sums[:, :s] + total
        total = total + sums[:, s:]
    w = jnp.exp(jnp.concatenate(parts, axis=1))
    if mask is not None:
        w = jnp.where(mask, w, 0.0)
    return _dot(w.astype(BF16), vb), total


def _sb_kernel(q_ref, k_ref, v_ref, o_ref, acc_ref, tot_ref):
    i = pl.program_id(2)
    t, s = SB_TILE, SB_SUB
    rows = lax.broadcasted_iota(jnp.int32, (s, s), 0)
    cols = lax.broadcasted_iota(jnp.int32, (s, s), 1)
    later = (rows > cols).astype(BF16)
    half = jnp.concatenate([later, jnp.ones((s, s), BF16)], axis=1)
    sum_mat = jnp.concatenate([half, half], axis=0)
    strictly_causal = (lax.broadcasted_iota(jnp.int32, (t, t), 1)
                       < lax.broadcasted_iota(jnp.int32, (t, t), 0))

    def visit(j, diagonal):
        start = pl.multiple_of(j * t, t)
        largest = None
        for h in range(SB_HEADS_PER_STEP):
            hs = slice(h * SB_HEAD_DIM, (h + 1) * SB_HEAD_DIM)
            total = jnp.zeros((t, s), F32) if diagonal else tot_ref[h]
            part, total = _sb_head_tile(
                q_ref[:, hs], k_ref[pl.ds(start, t), hs], v_ref[pl.ds(start, t), hs],
                sum_mat, total, strictly_causal if diagonal else None)
            acc_ref[h] = part if diagonal else acc_ref[h] + part
            tot_ref[h] = total
            top = jnp.max(total)
            largest = top if largest is None else jnp.maximum(largest, top)
        return largest

    def cond(carry):
        j, largest = carry
        return jnp.logical_and(j >= 0, largest > LOG_WEIGHT_FLOOR)

    def body(carry):
        j, _ = carry
        return j - 1, visit(j, False)

    lax.while_loop(cond, body, (i - 1, visit(i, True)))
    for h in range(SB_HEADS_PER_STEP):
        hs = slice(h * SB_HEAD_DIM, (h + 1) * SB_HEAD_DIM)
        o_ref[:, hs] = acc_ref[h].astype(o_ref.dtype)


def _sb_attention(q, k, v, batch, seq):
    t = SB_TILE
    nq = seq // t
    width = SB_HEADS_PER_STEP * SB_HEAD_DIM
    qspec = pl.BlockSpec((t, width), lambda b, g, i: (b * nq + i, g))
    kvspec = pl.BlockSpec((seq, width), lambda b, g, i: (b, g))
    return pl.pallas_call(
        _sb_kernel,
        out_shape=jax.ShapeDtypeStruct(q.shape, BF16),
        grid=(batch, SB_HEADS // SB_HEADS_PER_STEP, nq),
        in_specs=[qspec, kvspec, kvspec],
        out_specs=qspec,
        scratch_shapes=[pltpu.VMEM((SB_HEADS_PER_STEP, t, SB_HEAD_DIM), F32),
                        pltpu.VMEM((SB_HEADS_PER_STEP, t, SB_SUB), F32)],
        compiler_params=_params(3),
        name="sb_attention",
    )(q, k, v)


def _ssd_kernel(z_ref, xbc_ref, dt_ref, convw_ref, convb_ref, dtb_ref, alog_ref,
                dsk_ref, nrm_ref, o_ref, xpad_ref, xc_ref, st_ref, y_ref):
    n = SSM_CHUNK
    c = pl.program_id(1)

    @pl.when(c == 0)
    def _():
        xpad_ref[0:HALO, :] = jnp.zeros((HALO, SSM_CONV_DIM), F32)
        st_ref[...] = jnp.zeros_like(st_ref)

    @pl.when(c != 0)
    def _():
        xpad_ref[0:HALO, :] = xpad_ref[n:n + HALO, :]

    xpad_ref[HALO:HALO + n, :] = xbc_ref[...].astype(F32)

    first = HALO - (SSM_CONV - 1)
    for ct in range(SSM_CONV_DIM // CONV_COL_TILE):
        cs = slice(ct * CONV_COL_TILE, (ct + 1) * CONV_COL_TILE)
        acc = convb_ref[:, cs] + convw_ref[0:1, cs] * xpad_ref[first:first + n, cs]
        for i in range(1, SSM_CONV):
            acc = acc + convw_ref[i:i + 1, cs] * xpad_ref[first + i:first + i + n, cs]
        xc_ref[:, cs] = _silu(acc)

    rows = lax.broadcasted_iota(jnp.int32, (n, n), 0)
    cols = lax.broadcasted_iota(jnp.int32, (n, n), 1)
    causal = rows >= cols
    low = cols < SSM_HEAD_DIM
    low_row = low[0:1, :]

    dt = _softplus(dt_ref[...] + dtb_ref[...])
    d_a = dt * (-jnp.exp(alog_ref[...]))
    incl = causal.astype(BF16)
    cum = _dot(jnp.concatenate([incl] * 3, axis=1),
               jnp.concatenate(_split_bf16(d_a, 3), axis=0))
    cum_t = cum.T
    ecum = jnp.exp(cum)
    dt_t = dt.T
    w_t = (dt * jnp.exp(cum[n - 1:n, :] - cum)).T

    b_off = SSM_D_INNER
    c_off = SSM_D_INNER + SSM_GROUPS * SSM_STATE
    group_cache = {}

    def group(g):
        if g not in group_cache:
            bg = xc_ref[:, b_off + g * SSM_STATE:b_off + (g + 1) * SSM_STATE]
            cg = xc_ref[:, c_off + g * SSM_STATE:c_off + (g + 1) * SSM_STATE]
            cb = lax.dot_general(cg.astype(BF16), bg.astype(BF16),
                                 (((1,), (1,)), ((), ())), preferred_element_type=F32)
            group_cache[g] = (cb, cg, bg.T)
        return group_cache[g]

    for p in range(SSM_HEADS // 2):
        lanes = slice(p * LANES, (p + 1) * LANES)
        xs = xc_ref[:, lanes]
        state = st_ref[p]
        xs_pair = [jnp.where(low, xs, 0.0).astype(BF16), jnp.where(low, 0.0, xs).astype(BF16)]
        st_pair = [jnp.where(low, state, 0.0).astype(BF16),
                   jnp.where(low, 0.0, state).astype(BF16)]
        g_mats, c_mats, b_mats = [], [], []
        for h in (2 * p, 2 * p + 1):
            cb, cg, bg_t = group(h // SSM_HEADS_PER_GROUP)
            decay = jnp.where(causal, jnp.exp(cum[:, h:h + 1] - cum_t[h:h + 1, :]), 0.0)
            g_mats.append((cb * decay * dt_t[h:h + 1, :]).astype(BF16))
            c_mats.append((cg * ecum[:, h:h + 1]).astype(BF16))
            b_mats.append((bg_t * w_t[h:h + 1, :]).astype(BF16))
        y = _dot(jnp.concatenate(g_mats + c_mats, axis=1),
                 jnp.concatenate(xs_pair + st_pair, axis=0))
        y_ref[:, lanes] = y + dsk_ref[:, lanes] * xs
        last = ecum[n - 1:n, :]
        state_decay = jnp.where(low_row, last[:, 2 * p:2 * p + 1], last[:, 2 * p + 1:2 * p + 2])
        st_ref[p] = state * state_decay + _dot(jnp.concatenate(b_mats, axis=1),
                                               jnp.concatenate(xs_pair, axis=0))

    for gp in range(SSM_GROUPS // 2):
        tiles = [slice((3 * gp + i) * LANES, (3 * gp + i + 1) * LANES) for i in range(3)]
        ys = [y_ref[:, s] * _silu(z_ref[:, s].astype(F32)) for s in tiles]
        sq = [v * v for v in ys]
        rsum = lambda v: jnp.sum(v, axis=-1, keepdims=True)
        ss_a = rsum(sq[0]) + rsum(jnp.where(low, sq[1], 0.0))
        ss_b = rsum(sq[2]) + rsum(jnp.where(low, 0.0, sq[1]))
        r_a = lax.rsqrt(ss_a * (1.0 / SSM_GROUP_WIDTH) + NORM_EPS)
        r_b = lax.rsqrt(ss_b * (1.0 / SSM_GROUP_WIDTH) + NORM_EPS)
        scales = [r_a, jnp.where(low, r_a, r_b), r_b]
        for s, v, r in zip(tiles, ys, scales):
            o_ref[:, s] = (v * r * nrm_ref[:, s]).astype(o_ref.dtype)


def _ssd(z, xbc, dt, conv_w, conv_b, dt_bias, a_log, d_skip, ssm_norm, batch, seq):
    n = SSM_CHUNK
    nc = seq // n
    pad = lambda v: jnp.pad(v, (0, LANES - SSM_HEADS)).reshape(1, LANES)
    row = lambda w: pl.BlockSpec((n, w), lambda b, c: (b * nc + c, 0))
    return pl.pallas_call(
        _ssd_kernel,
        out_shape=jax.ShapeDtypeStruct(z.shape, BF16),
        grid=(batch, nc),
        in_specs=[row(SSM_D_INNER), row(SSM_CONV_DIM), row(LANES),
                  _resident((SSM_CONV, SSM_CONV_DIM)), _resident((1, SSM_CONV_DIM)),
                  _resident((1, LANES)), _resident((1, LANES)),
                  _resident((1, SSM_D_INNER)), _resident((1, SSM_D_INNER))],
        out_specs=row(SSM_D_INNER),
        scratch_shapes=[
            pltpu.VMEM((HALO + n, SSM_CONV_DIM), F32),
            pltpu.VMEM((n, SSM_CONV_DIM), F32),
            pltpu.VMEM((SSM_HEADS // 2, SSM_STATE, LANES), F32),
            pltpu.VMEM((n, SSM_D_INNER), F32),
        ],
        compiler_params=_params(2),
        name="ssd",
    )(z, xbc, dt, conv_w, conv_b.reshape(1, -1), pad(dt_bias), pad(a_log),
      jnp.repeat(d_skip, SSM_HEAD_DIM).reshape(1, -1), ssm_norm.reshape(1, -1))


def _merge_kernel(x_ref, ysb_ref, yssm_ref, gs_ref, gm_ref, wsb_ref, wssm_ref, wo_ref, o_ref):
    y_sb = _dot(ysb_ref[...], wsb_ref[...])
    y_ssm = _dot(yssm_ref[...], wssm_ref[...])
    merged = (jax.nn.sigmoid(gs_ref[...].astype(F32)) * y_sb
              + jax.nn.sigmoid(gm_ref[...].astype(F32)) * y_ssm)
    o_ref[...] = x_ref[...] + _dot(merged.astype(BF16), wo_ref[...])


def _merge(x, y_sb, y_ssm, gate_sb, gate_ssm, w_sb, w_ssm, w_out):
    t, d = x.shape
    tm = MERGE_TOKEN_TILE
    row = lambda w: pl.BlockSpec((tm, w), lambda i: (i, 0))
    return pl.pallas_call(
        _merge_kernel,
        out_shape=jax.ShapeDtypeStruct((t, d), F32),
        grid=(t // tm,),
        in_specs=[row(d), row(SB_WIDTH), row(SSM_D_INNER), row(d), row(d),
                  _resident((SB_WIDTH, d)), _resident((SSM_D_INNER, d)), _resident((d, d))],
        out_specs=row(d),
        compiler_params=_params(1),
        name="merge",
    )(x, y_sb, y_ssm, gate_sb, gate_ssm, w_sb.astype(BF16), w_ssm.astype(BF16),
      w_out.astype(BF16))


def kernel(x, ffn1_norm, ffn1_w_up, ffn1_w_down, mix_norm, w_in, conv_w, conv_b, dt_bias,
           a_log, d_skip, ssm_norm, w_branch_sb, w_branch_ssm, w_out, ffn2_norm, ffn2_w_up,
           ffn2_w_down, final_norm):
    batch, seq, d = x.shape
    depth = w_in.shape[0]
    assert d == D_MODEL and seq % SB_TILE == 0 and seq % SSM_CHUNK == 0
    assert (batch * seq) % FFN_TOKEN_TILE == 0
    xf = x.reshape(batch * seq, d)
    for l in range(depth):
        xf = _ffn(xf, ffn1_norm[l], ffn1_w_up[l], ffn1_w_down[l], None)
        q, k, v, z, xbc, dt, gate_sb, gate_ssm = _in_proj(xf, mix_norm[l], w_in[l])
        y_sb = _sb_attention(q, k, v, batch, seq)
        y_ssm = _ssd(z, xbc, dt, conv_w[l], conv_b[l], dt_bias[l], a_log[l], d_skip[l],
                     ssm_norm[l], batch, seq)
        xf = _merge(xf, y_sb, y_ssm, gate_sb, gate_ssm, w_branch_sb[l], w_branch_ssm[l],
                    w_out[l])
        xf = _ffn(xf, ffn2_norm[l], ffn2_w_up[l], ffn2_w_down[l],
                  final_norm if l == depth - 1 else None)
    return xf.reshape(batch, seq, d)
```

```python
import functools

import jax
import jax.numpy as jnp
from jax import lax
from jax.experimental import pallas as pl
from jax.experimental.pallas import tpu as pltpu

F32 = jnp.float32
BF16 = jnp.bfloat16

D_MODEL = 1024
SB_HEADS = 4
SB_HEAD_DIM = 128
SB_WIDTH = SB_HEADS * SB_HEAD_DIM
SSM_D_INNER = 1536
SSM_HEAD_DIM = 64
SSM_HEADS = SSM_D_INNER // SSM_HEAD_DIM
SSM_GROUPS = 8
SSM_HEADS_PER_GROUP = SSM_HEADS // SSM_GROUPS
SSM_STATE = 128
SSM_CONV = 4
SSM_CONV_DIM = SSM_D_INNER + 2 * SSM_GROUPS * SSM_STATE
SSM_GROUP_WIDTH = SSM_D_INNER // SSM_GROUPS
FFN_HIDDEN = 2816
NORM_EPS = 1e-6

LANES = 128
SUBLANES = 8
MXU_DIM = 256
VMEM_LIMIT = 56 * 1024 * 1024

FFN_TOKEN_TILE = 512
FFN_HIDDEN_CHUNKS = (5 * MXU_DIM, 6 * MXU_DIM)
PROJ_TOKEN_TILE = 256
MERGE_TOKEN_TILE = 512
SB_TILE = 256
SB_SUB = 128
SB_HEADS_PER_STEP = 2
LOG_WEIGHT_FLOOR = -105.0
SSM_CHUNK = 128
SSM_CHUNKS_PER_STEP = 1
CONV_COL_TILE = 256
HALO = SUBLANES

assert sum(FFN_HIDDEN_CHUNKS) == FFN_HIDDEN


def _params(n_axes, flags=None):
    return pltpu.CompilerParams(
        dimension_semantics=("arbitrary",) * n_axes, vmem_limit_bytes=VMEM_LIMIT, flags=flags)


def _resident(shape):
    zeros = (0,) * len(shape)
    return pl.BlockSpec(shape, lambda *_: zeros, pipeline_mode=pl.Buffered(1))


def _rms_norm(x, gain):
    ms = jnp.mean(x * x, axis=-1, keepdims=True)
    return x * lax.rsqrt(ms + NORM_EPS) * gain


def _silu(x):
    return x * jax.nn.sigmoid(x)


def _softplus(x):
    return jnp.maximum(x, 0.0) + jnp.log1p(jnp.exp(-jnp.abs(x)))


def _dot(a, b):
    return jnp.dot(a, b, preferred_element_type=F32)


def _split_bf16(x, parts):
    out = []
    for _ in range(parts - 1):
        hi = x.astype(BF16)
        out.append(hi)
        x = x - hi.astype(F32)
    out.append(x.astype(BF16))
    return out


def _ffn_kernel(x_ref, g_ref, wg_ref, wu_ref, wd_ref, *rest, final_norm):
    if final_norm:
        fg_ref, o_ref = rest
    else:
        (o_ref,) = rest
    x = x_ref[...]
    xn = _rms_norm(x, g_ref[...]).astype(BF16)
    acc, start = None, 0
    for width in FFN_HIDDEN_CHUNKS:
        cols = slice(start, start + width)
        start += width
        gate = _dot(xn, wg_ref[:, cols])
        up = _dot(xn, wu_ref[:, cols])
        act = (_silu(gate) * up).astype(BF16)
        part = _dot(act, wd_ref[cols, :])
        acc = part if acc is None else acc + part
    y = x + 0.5 * acc
    if final_norm:
        y = _rms_norm(y, fg_ref[...])
    o_ref[...] = y


def _ffn(x, gain, w_up, w_down, final_gain):
    t, d = x.shape
    tm = FFN_TOKEN_TILE
    w_gate = w_up[:, :FFN_HIDDEN].astype(BF16)
    w_lin = w_up[:, FFN_HIDDEN:].astype(BF16)
    w_dn = w_down.astype(BF16)
    row = pl.BlockSpec((tm, d), lambda i: (i, 0))
    in_specs = [row, _resident((1, d)), _resident((d, FFN_HIDDEN)),
                _resident((d, FFN_HIDDEN)), _resident((FFN_HIDDEN, d))]
    args = [x, gain.reshape(1, d), w_gate, w_lin, w_dn]
    if final_gain is not None:
        in_specs.append(_resident((1, d)))
        args.append(final_gain.reshape(1, d))
    return pl.pallas_call(
        functools.partial(_ffn_kernel, final_norm=final_gain is not None),
        out_shape=jax.ShapeDtypeStruct((t, d), F32),
        grid=(t // tm,),
        in_specs=in_specs,
        out_specs=row,
        compiler_params=_params(1),
        name="ffn",
    )(*args)


def _in_proj_kernel(x_ref, g_ref, wq_ref, wk_ref, wv_ref, wz_ref, wx_ref, wdt_ref,
                    wgs_ref, wgm_ref, convw_ref, convb_ref, q_ref, k_ref, v_ref, z_ref,
                    xc_ref, dt_ref, gs_ref, gm_ref, xpad_ref, *, tiles_per_seq):
    tm = PROJ_TOKEN_TILE

    @pl.when(pl.program_id(0) == 0)
    def _():
        xpad_ref[...] = jnp.zeros_like(xpad_ref)

    xn = _rms_norm(x_ref[...], g_ref[...]).astype(BF16)
    starts_seq = lax.rem(pl.program_id(0), tiles_per_seq) == 0

    def project(w_ref, o_ref, scale=None):
        def tile(cs):
            y = _dot(xn, w_ref[:, cs])
            o_ref[:, cs] = (y if scale is None else y * scale).astype(o_ref.dtype)
        width = o_ref.shape[1]
        step = min(width, CONV_COL_TILE)
        return [functools.partial(tile, slice(c, c + step)) for c in range(0, width, step)]

    plain = (project(wq_ref, q_ref, SB_HEAD_DIM ** -0.5) + project(wk_ref, k_ref)
             + project(wv_ref, v_ref) + project(wz_ref, z_ref) + project(wdt_ref, dt_ref)
             + project(wgs_ref, gs_ref) + project(wgm_ref, gm_ref))

    n_tiles = SSM_CONV_DIM // CONV_COL_TILE
    for ct in range(n_tiles):
        cs = slice(ct * CONV_COL_TILE, (ct + 1) * CONV_COL_TILE)
        y = _dot(xn, wx_ref[:, cs])
        tail = jnp.where(starts_seq, 0.0, xpad_ref[:, cs])
        xpad_ref[:, cs] = y[tm - HALO:tm, :]
        for task in plain[ct::n_tiles]:
            task()
        ext = jnp.concatenate([tail, y], axis=0)
        acc = convb_ref[:, cs]
        for i in range(SSM_CONV - 1):
            shifted = pltpu.roll(ext, SSM_CONV - 1 - i, axis=0)[HALO:, :]
            acc = acc + convw_ref[i:i + 1, cs] * shifted
        acc = acc + convw_ref[SSM_CONV - 1:SSM_CONV, cs] * y
        xc_ref[:, cs] = _silu(acc).astype(BF16)


def _in_proj(x, gain, w_in, conv_w, conv_b, seq):
    t, d = x.shape
    tm = PROJ_TOKEN_TILE
    widths = (SB_WIDTH, SB_WIDTH, SB_WIDTH, SSM_D_INNER, SSM_CONV_DIM, SSM_HEADS,
              D_MODEL, D_MODEL)
    pieces, start = [], 0
    for w in widths:
        pieces.append(w_in[:, start:start + w].astype(BF16))
        start += w
    pieces[5] = jnp.pad(pieces[5], ((0, 0), (0, LANES - SSM_HEADS)))
    out_widths = (SB_WIDTH, SB_WIDTH, SB_WIDTH, SSM_D_INNER, SSM_CONV_DIM, LANES,
                  D_MODEL, D_MODEL)
    out_dtypes = (BF16, BF16, BF16, BF16, BF16, F32, BF16, BF16)
    row = lambda w: pl.BlockSpec((tm, w), lambda i: (i, 0))
    return pl.pallas_call(
        functools.partial(_in_proj_kernel, tiles_per_seq=seq // tm),
        out_shape=[jax.ShapeDtypeStruct((t, w), dt) for w, dt in zip(out_widths, out_dtypes)],
        grid=(t // tm,),
        in_specs=([row(d), _resident((1, d))] + [_resident(p.shape) for p in pieces]
                  + [_resident((SSM_CONV, SSM_CONV_DIM)), _resident((1, SSM_CONV_DIM))]),
        out_specs=[row(w) for w in out_widths],
        scratch_shapes=[pltpu.VMEM((HALO, SSM_CONV_DIM), F32)],
        compiler_params=_params(1),
        name="in_proj",
    )(x, gain.reshape(1, d), *pieces, conv_w, conv_b.reshape(1, -1))


def _sb_rows(z, vb, sum_mat, total, blocks):
    s = SB_SUB
    weights = {}
    for b, mask in blocks:
        zb = z[:, b * s:(b + 1) * s]
        tail = jnp.log(1.0 + jnp.exp(-jnp.abs(zb)))
        log_keep = -(jnp.maximum(zb, 0.0) + tail)
        if mask is not None:
            log_keep = jnp.where(mask, log_keep, 0.0)
        sums = _dot(jnp.concatenate(_split_bf16(log_keep, 2), axis=1), sum_mat)
        w = jnp.exp(jnp.minimum(zb, 0.0) - tail + sums[:, :s] + total)
        if mask is not None:
            w = jnp.where(mask, w, 0.0)
        weights[b] = w.astype(BF16)
        total = total + sums[:, s:]
    order = sorted(weights)
    w_all = jnp.concatenate([weights[b] for b in order], axis=1)
    return _dot(w_all, vb[order[0] * s:(order[-1] + 1) * s, :]), total


def _sb_kernel(q_ref, k_ref, v_ref, o_ref, acc_ref, tot_ref):
    i = pl.program_id(2)
    t, s = SB_TILE, SB_SUB
    rows = lax.broadcasted_iota(jnp.int32, (s, s), 0)
    cols = lax.broadcasted_iota(jnp.int32, (s, s), 1)
    later = (rows > cols).astype(BF16)
    half = jnp.concatenate([later, jnp.ones((s, s), BF16)], axis=1)
    sum_mat = jnp.concatenate([half, half], axis=0)
    before = cols < rows

    def visit(j, diagonal):
        start = pl.multiple_of(j * t, t)
        largest = None
        for h in range(SB_HEADS_PER_STEP):
            hs = slice(h * SB_HEAD_DIM, (h + 1) * SB_HEAD_DIM)
            vb = v_ref[pl.ds(start, t), hs]
            z = lax.dot_general(q_ref[:, hs], k_ref[pl.ds(start, t), hs],
                                (((1,), (1,)), ((), ())), preferred_element_type=F32)
            if diagonal:
                zero = jnp.zeros((s, s), F32)
                top, top_tot = _sb_rows(z[:s], vb, sum_mat, zero, [(0, before)])
                bot, bot_tot = _sb_rows(z[s:], vb, sum_mat, zero, [(1, before), (0, None)])
                acc_ref[h] = jnp.concatenate([top, bot], axis=0)
                total = jnp.concatenate([top_tot, bot_tot], axis=0)
            else:
                part, total = _sb_rows(z, vb, sum_mat, tot_ref[h], [(1, None), (0, None)])
                acc_ref[h] = acc_ref[h] + part
            tot_ref[h] = total
            top = jnp.max(total)
            largest = top if largest is None else jnp.maximum(largest, top)
        return largest

    def cond(carry):
        j, largest = carry
        return jnp.logical_and(j >= 0, largest > LOG_WEIGHT_FLOOR)

    def body(carry):
        j, _ = carry
        return j - 1, visit(j, False)

    lax.while_loop(cond, body, (i - 1, visit(i, True)))
    for h in range(SB_HEADS_PER_STEP):
        hs = slice(h * SB_HEAD_DIM, (h + 1) * SB_HEAD_DIM)
        o_ref[:, hs] = acc_ref[h].astype(o_ref.dtype)


def _sb_attention(q, k, v, batch, seq):
    t = SB_TILE
    nq = seq // t
    width = SB_HEADS_PER_STEP * SB_HEAD_DIM
    qspec = pl.BlockSpec((t, width), lambda b, g, i: (b * nq + i, g))
    kvspec = pl.BlockSpec((seq, width), lambda b, g, i: (b, g))
    return pl.pallas_call(
        _sb_kernel,
        out_shape=jax.ShapeDtypeStruct(q.shape, BF16),
        grid=(batch, SB_HEADS // SB_HEADS_PER_STEP, nq),
        in_specs=[qspec, kvspec, kvspec],
        out_specs=qspec,
        scratch_shapes=[pltpu.VMEM((SB_HEADS_PER_STEP, t, SB_HEAD_DIM), F32),
                        pltpu.VMEM((SB_HEADS_PER_STEP, t, SB_SUB), F32)],
        compiler_params=_params(3),
        name="sb_attention",
    )(q, k, v)


def _ssd_kernel(xc_ref, dt_ref, dtb_ref, alog_ref, dsk_ref, o_ref, st_ref):
    n = SSM_CHUNK

    @pl.when(pl.program_id(1) == 0)
    def _():
        st_ref[...] = jnp.zeros_like(st_ref)

    rows = lax.broadcasted_iota(jnp.int32, (n, n), 0)
    cols = lax.broadcasted_iota(jnp.int32, (n, n), 1)
    causal = rows >= cols
    incl = causal.astype(BF16)
    low = cols < SSM_HEAD_DIM
    low_row = low[0:1, :]
    neg_a = -jnp.exp(alog_ref[...])
    b_off = SSM_D_INNER
    c_off = SSM_D_INNER + SSM_GROUPS * SSM_STATE

    def scalars(rs):
        dt = _softplus(dt_ref[rs, :] + dtb_ref[...])
        cum = _dot(jnp.concatenate([incl] * 3, axis=1),
                   jnp.concatenate(_split_bf16(dt * neg_a, 3), axis=0))
        return dict(cum=cum, cum_t=cum.T, ecum=jnp.exp(cum), dt_t=dt.T,
                    w_t=(dt * jnp.exp(cum[n - 1:n, :] - cum)).T)

    chunks = [slice(c * n, (c + 1) * n) for c in range(SSM_CHUNKS_PER_STEP)]
    per_chunk = [scalars(rs) for rs in chunks]
    group_cache = {}

    def group(c, g):
        if (c, g) not in group_cache:
            bg = xc_ref[chunks[c], b_off + g * SSM_STATE:b_off + (g + 1) * SSM_STATE]
            cg = xc_ref[chunks[c], c_off + g * SSM_STATE:c_off + (g + 1) * SSM_STATE]
            cb = lax.dot_general(cg, bg, (((1,), (1,)), ((), ())), preferred_element_type=F32)
            group_cache[c, g] = (cb, cg.astype(F32), bg.astype(F32).T)
        return group_cache[c, g]

    for p in range(SSM_HEADS // 2):
        lanes = slice(p * LANES, (p + 1) * LANES)
        state = st_ref[p]
        for c, rs in enumerate(chunks):
            s = per_chunk[c]
            xs = xc_ref[rs, lanes].astype(F32)
            xs_pair = [jnp.where(low, xs, 0.0).astype(BF16), jnp.where(low, 0.0, xs).astype(BF16)]
            st_pair = [jnp.where(low, state, 0.0).astype(BF16),
                       jnp.where(low, 0.0, state).astype(BF16)]
            g_mats, c_mats, b_mats = [], [], []
            for h in (2 * p, 2 * p + 1):
                cb, cg, bg_t = group(c, h // SSM_HEADS_PER_GROUP)
                decay = jnp.where(
                    causal, jnp.exp(s["cum"][:, h:h + 1] - s["cum_t"][h:h + 1, :]), 0.0)
                g_mats.append((cb * decay * s["dt_t"][h:h + 1, :]).astype(BF16))
                c_mats.append((cg * s["ecum"][:, h:h + 1]).astype(BF16))
                b_mats.append((bg_t * s["w_t"][h:h + 1, :]).astype(BF16))
            y = _dot(jnp.concatenate(g_mats + c_mats, axis=1),
                     jnp.concatenate(xs_pair + st_pair, axis=0))
            o_ref[rs, lanes] = (y + dsk_ref[:, lanes] * xs).astype(o_ref.dtype)
            last = s["ecum"][n - 1:n, :]
            state_decay = jnp.where(low_row, last[:, 2 * p:2 * p + 1],
                                    last[:, 2 * p + 1:2 * p + 2])
            state = state * state_decay + _dot(jnp.concatenate(b_mats, axis=1),
                                               jnp.concatenate(xs_pair, axis=0))
        st_ref[p] = state


def _ssd(xc, dt, dt_bias, a_log, d_skip, batch, seq):
    n = SSM_CHUNK * SSM_CHUNKS_PER_STEP
    nc = seq // n
    pad = lambda v: jnp.pad(v, (0, LANES - SSM_HEADS)).reshape(1, LANES)
    row = lambda w: pl.BlockSpec((n, w), lambda b, c: (b * nc + c, 0))
    return pl.pallas_call(
        _ssd_kernel,
        out_shape=jax.ShapeDtypeStruct((batch * seq, SSM_D_INNER), BF16),
        grid=(batch, nc),
        in_specs=[row(SSM_CONV_DIM), row(LANES), _resident((1, LANES)), _resident((1, LANES)),
                  _resident((1, SSM_D_INNER))],
        out_specs=row(SSM_D_INNER),
        scratch_shapes=[pltpu.VMEM((SSM_HEADS // 2, SSM_STATE, LANES), F32)],
        compiler_params=_params(2),
        name="ssd",
    )(xc, dt, pad(dt_bias), pad(a_log), jnp.repeat(d_skip, SSM_HEAD_DIM).reshape(1, -1))


def _merge_kernel(x_ref, ysb_ref, y_ref, z_ref, gs_ref, gm_ref, nrm_ref, wsb_ref, wssm_ref,
                  wo_ref, o_ref, yn_ref):
    tm = MERGE_TOKEN_TILE
    low = lax.broadcasted_iota(jnp.int32, (tm, LANES), 1) < SSM_HEAD_DIM
    rsum = lambda v: jnp.sum(v, axis=-1, keepdims=True)
    y_sb = _dot(ysb_ref[...], wsb_ref[...])
    y_ssm = None
    pairs_per_part = SSM_GROUPS // 4
    for gp in range(SSM_GROUPS // 2):
        tiles = [slice((3 * gp + i) * LANES, (3 * gp + i + 1) * LANES) for i in range(3)]
        ys = [y_ref[:, s].astype(F32) * _silu(z_ref[:, s].astype(F32)) for s in tiles]
        sq = [v * v for v in ys]
        ss_a = rsum(sq[0]) + rsum(jnp.where(low, sq[1], 0.0))
        ss_b = rsum(sq[2]) + rsum(jnp.where(low, 0.0, sq[1]))
        r_a = lax.rsqrt(ss_a * (1.0 / SSM_GROUP_WIDTH) + NORM_EPS)
        r_b = lax.rsqrt(ss_b * (1.0 / SSM_GROUP_WIDTH) + NORM_EPS)
        scales = [r_a, jnp.where(low, r_a, r_b), r_b]
        for s, v, r in zip(tiles, ys, scales):
            yn_ref[:, s] = (v * r * nrm_ref[:, s]).astype(BF16)
        if (gp + 1) % pairs_per_part == 0:
            part = slice((gp + 1 - pairs_per_part) * 3 * LANES, (gp + 1) * 3 * LANES)
            term = _dot(yn_ref[:, part], wssm_ref[part, :])
            y_ssm = term if y_ssm is None else y_ssm + term
    merged = (jax.nn.sigmoid(gs_ref[...].astype(F32)) * y_sb
              + jax.nn.sigmoid(gm_ref[...].astype(F32)) * y_ssm)
    o_ref[...] = x_ref[...] + _dot(merged.astype(BF16), wo_ref[...])


def _merge(x, y_sb, y_scan, z, gate_sb, gate_ssm, ssm_norm, w_sb, w_ssm, w_out):
    t, d = x.shape
    tm = MERGE_TOKEN_TILE
    row = lambda w: pl.BlockSpec((tm, w), lambda i: (i, 0))
    return pl.pallas_call(
        _merge_kernel,
        out_shape=jax.ShapeDtypeStruct((t, d), F32),
        grid=(t // tm,),
        in_specs=[row(d), row(SB_WIDTH), row(SSM_D_INNER), row(SSM_D_INNER), row(d), row(d),
                  _resident((1, SSM_D_INNER)), _resident((SB_WIDTH, d)),
                  _resident((SSM_D_INNER, d)), _resident((d, d))],
        out_specs=row(d),
        scratch_shapes=[pltpu.VMEM((tm, SSM_D_INNER), BF16)],
        compiler_params=_params(1),
        name="merge",
    )(x, y_sb, y_scan, z, gate_sb, gate_ssm, ssm_norm.reshape(1, -1), w_sb.astype(BF16),
      w_ssm.astype(BF16), w_out.astype(BF16))


def kernel(x, ffn1_norm, ffn1_w_up, ffn1_w_down, mix_norm, w_in, conv_w, conv_b, dt_bias,
           a_log, d_skip, ssm_norm, w_branch_sb, w_branch_ssm, w_out, ffn2_norm, ffn2_w_up,
           ffn2_w_down, final_norm):
    batch, seq, d = x.shape
    depth = w_in.shape[0]
    assert d == D_MODEL and seq % SB_TILE == 0 and seq % (SSM_CHUNK * SSM_CHUNKS_PER_STEP) == 0
    assert seq % PROJ_TOKEN_TILE == 0 and (batch * seq) % FFN_TOKEN_TILE == 0
    xf = x.reshape(batch * seq, d)
    for l in range(depth):
        xf = _ffn(xf, ffn1_norm[l], ffn1_w_up[l], ffn1_w_down[l], None)
        q, k, v, z, xc, dt, gate_sb, gate_ssm = _in_proj(xf, mix_norm[l], w_in[l], conv_w[l],
                                                         conv_b[l], seq)
        y_sb = _sb_attention(q, k, v, batch, seq)
        y_scan = _ssd(xc, dt, dt_bias[l], a_log[l], d_skip[l], batch, seq)
        xf = _merge(xf, y_sb, y_scan, z, gate_sb, gate_ssm, ssm_norm[l], w_branch_sb[l],
                    w_branch_ssm[l], w_out[l])
        xf = _ffn(xf, ffn2_norm[l], ffn2_w_up[l], ffn2_w_down[l],
                  final_norm if l == depth - 1 else None)
    return xf.reshape(batch, seq, d)
```

```python
import functools

import jax
import jax.numpy as jnp
from jax import lax
from jax.experimental import pallas as pl
from jax.experimental.pallas import tpu as pltpu

F32 = jnp.float32
BF16 = jnp.bfloat16

D_MODEL = 1024
SB_HEADS = 4
SB_HEAD_DIM = 128
SB_WIDTH = SB_HEADS * SB_HEAD_DIM
SSM_D_INNER = 1536
SSM_HEAD_DIM = 64
SSM_HEADS = SSM_D_INNER // SSM_HEAD_DIM
SSM_GROUPS = 8
SSM_HEADS_PER_GROUP = SSM_HEADS // SSM_GROUPS
SSM_STATE = 128
SSM_CONV = 4
SSM_CONV_DIM = SSM_D_INNER + 2 * SSM_GROUPS * SSM_STATE
SSM_GROUP_WIDTH = SSM_D_INNER // SSM_GROUPS
FFN_HIDDEN = 2816
NORM_EPS = 1e-6

LANES = 128
SUBLANES = 8
MXU_DIM = 256
VMEM_LIMIT = 56 * 1024 * 1024

FFN_TOKEN_TILE = 512
FFN_HIDDEN_CHUNKS = (5 * MXU_DIM, 6 * MXU_DIM)
PROJ_TOKEN_TILE = 256
MERGE_TOKEN_TILE = 512
SB_TILE = 256
SB_SUB = 128
SB_HEADS_PER_STEP = 2
SB_QTILES_PER_STEP = 2
LOG_WEIGHT_FLOOR = -105.0
DEAD_TOTAL = -1e30
SSM_CHUNK = 128
SSM_CHUNKS_PER_STEP = 1
SCAN_SCALARS = 5
CONV_COL_TILE = 256
HALO = SUBLANES

assert sum(FFN_HIDDEN_CHUNKS) == FFN_HIDDEN


def _params(n_axes, flags=None):
    return pltpu.CompilerParams(
        dimension_semantics=("arbitrary",) * n_axes, vmem_limit_bytes=VMEM_LIMIT, flags=flags)


def _resident(shape):
    zeros = (0,) * len(shape)
    return pl.BlockSpec(shape, lambda *_: zeros, pipeline_mode=pl.Buffered(1))


def _rms_norm(x, gain):
    ms = jnp.mean(x * x, axis=-1, keepdims=True)
    return x * lax.rsqrt(ms + NORM_EPS) * gain


def _silu(x):
    return x * jax.nn.sigmoid(x)


def _softplus(x):
    return jnp.maximum(x, 0.0) + jnp.log1p(jnp.exp(-jnp.abs(x)))


def _dot(a, b):
    return jnp.dot(a, b, preferred_element_type=F32)


def _split_bf16(x, parts):
    out = []
    for _ in range(parts - 1):
        hi = x.astype(BF16)
        out.append(hi)
        x = x - hi.astype(F32)
    out.append(x.astype(BF16))
    return out


def _ffn_kernel(x_ref, g_ref, wg_ref, wu_ref, wd_ref, *rest, final_norm):
    if final_norm:
        fg_ref, o_ref = rest
    else:
        (o_ref,) = rest
    x = x_ref[...]
    xn = _rms_norm(x, g_ref[...]).astype(BF16)
    acc, start = None, 0
    for width in FFN_HIDDEN_CHUNKS:
        cols = slice(start, start + width)
        start += width
        gate = _dot(xn, wg_ref[:, cols])
        up = _dot(xn, wu_ref[:, cols])
        act = (_silu(gate) * up).astype(BF16)
        part = _dot(act, wd_ref[cols, :])
        acc = part if acc is None else acc + part
    y = x + 0.5 * acc
    if final_norm:
        y = _rms_norm(y, fg_ref[...])
    o_ref[...] = y


def _ffn(x, gain, w_up, w_down, final_gain):
    t, d = x.shape
    tm = FFN_TOKEN_TILE
    w_gate = w_up[:, :FFN_HIDDEN].astype(BF16)
    w_lin = w_up[:, FFN_HIDDEN:].astype(BF16)
    w_dn = w_down.astype(BF16)
    row = pl.BlockSpec((tm, d), lambda i: (i, 0))
    in_specs = [row, _resident((1, d)), _resident((d, FFN_HIDDEN)),
                _resident((d, FFN_HIDDEN)), _resident((FFN_HIDDEN, d))]
    args = [x, gain.reshape(1, d), w_gate, w_lin, w_dn]
    if final_gain is not None:
        in_specs.append(_resident((1, d)))
        args.append(final_gain.reshape(1, d))
    return pl.pallas_call(
        functools.partial(_ffn_kernel, final_norm=final_gain is not None),
        out_shape=jax.ShapeDtypeStruct((t, d), F32),
        grid=(t // tm,),
        in_specs=in_specs,
        out_specs=row,
        compiler_params=_params(1),
        name="ffn",
    )(*args)


def _in_proj_kernel(x_ref, g_ref, wq_ref, wk_ref, wv_ref, wz_ref, wx_ref, wdt_ref,
                    wgs_ref, wgm_ref, convw_ref, convb_ref, dtb_ref, alog_ref, q_ref, k_ref,
                    v_ref, z_ref, xc_ref, sc_ref, gs_ref, gm_ref, xpad_ref, *, tiles_per_seq):
    tm = PROJ_TOKEN_TILE
    n = SSM_CHUNK

    @pl.when(pl.program_id(0) == 0)
    def _():
        xpad_ref[...] = jnp.zeros_like(xpad_ref)

    xn = _rms_norm(x_ref[...], g_ref[...]).astype(BF16)
    starts_seq = lax.rem(pl.program_id(0), tiles_per_seq) == 0

    def project(w_ref, o_ref, scale=None):
        def tile(cs):
            y = _dot(xn, w_ref[:, cs])
            o_ref[:, cs] = (y if scale is None else y * scale).astype(o_ref.dtype)
        width = o_ref.shape[1]
        step = min(width, CONV_COL_TILE)
        return [functools.partial(tile, slice(c, c + step)) for c in range(0, width, step)]

    def scan_scalars():
        dt_all = _softplus(_dot(xn, wdt_ref[...]) + dtb_ref[...])
        neg_a = -jnp.exp(alog_ref[...])
        incl = (lax.broadcasted_iota(jnp.int32, (n, n), 0)
                >= lax.broadcasted_iota(jnp.int32, (n, n), 1)).astype(BF16)
        incl3 = jnp.concatenate([incl] * 3, axis=1)
        for c in range(tm // n):
            rs = slice(c * n, (c + 1) * n)
            dt = dt_all[rs, :]
            cum = _dot(incl3, jnp.concatenate(_split_bf16(dt * neg_a, 3), axis=0))
            parts = [cum, cum.T, jnp.exp(cum), dt.T, (dt * jnp.exp(cum[n - 1:n, :] - cum)).T]
            for k, part in enumerate(parts):
                sc_ref[rs, k * LANES:(k + 1) * LANES] = part

    plain = ([scan_scalars] + project(wq_ref, q_ref, SB_HEAD_DIM ** -0.5)
             + project(wk_ref, k_ref) + project(wv_ref, v_ref) + project(wz_ref, z_ref)
             + project(wgs_ref, gs_ref) + project(wgm_ref, gm_ref))

    n_tiles = SSM_CONV_DIM // CONV_COL_TILE
    for ct in range(n_tiles):
        cs = slice(ct * CONV_COL_TILE, (ct + 1) * CONV_COL_TILE)
        y = _dot(xn, wx_ref[:, cs])
        tail = jnp.where(starts_seq, 0.0, xpad_ref[:, cs])
        xpad_ref[:, cs] = y[tm - HALO:tm, :]
        for task in plain[ct::n_tiles]:
            task()
        ext = jnp.concatenate([tail, y], axis=0)
        acc = convb_ref[:, cs]
        for i in range(SSM_CONV - 1):
            shifted = pltpu.roll(ext, SSM_CONV - 1 - i, axis=0)[HALO:, :]
            acc = acc + convw_ref[i:i + 1, cs] * shifted
        acc = acc + convw_ref[SSM_CONV - 1:SSM_CONV, cs] * y
        xc_ref[:, cs] = _silu(acc).astype(BF16)


def _in_proj(x, gain, w_in, conv_w, conv_b, dt_bias, a_log, seq):
    t, d = x.shape
    tm = PROJ_TOKEN_TILE
    pad = lambda v: jnp.pad(v, (0, LANES - SSM_HEADS)).reshape(1, LANES)
    widths = (SB_WIDTH, SB_WIDTH, SB_WIDTH, SSM_D_INNER, SSM_CONV_DIM, SSM_HEADS,
              D_MODEL, D_MODEL)
    pieces, start = [], 0
    for w in widths:
        pieces.append(w_in[:, start:start + w].astype(BF16))
        start += w
    pieces[5] = jnp.pad(pieces[5], ((0, 0), (0, LANES - SSM_HEADS)))
    out_widths = (SB_WIDTH, SB_WIDTH, SB_WIDTH, SSM_D_INNER, SSM_CONV_DIM, SCAN_SCALARS * LANES,
                  D_MODEL, D_MODEL)
    out_dtypes = (BF16, BF16, BF16, BF16, BF16, F32, BF16, BF16)
    row = lambda w: pl.BlockSpec((tm, w), lambda i: (i, 0))
    return pl.pallas_call(
        functools.partial(_in_proj_kernel, tiles_per_seq=seq // tm),
        out_shape=[jax.ShapeDtypeStruct((t, w), dt) for w, dt in zip(out_widths, out_dtypes)],
        grid=(t // tm,),
        in_specs=([row(d), _resident((1, d))] + [_resident(p.shape) for p in pieces]
                  + [_resident((SSM_CONV, SSM_CONV_DIM)), _resident((1, SSM_CONV_DIM)),
                     _resident((1, LANES)), _resident((1, LANES))]),
        out_specs=[row(w) for w in out_widths],
        scratch_shapes=[pltpu.VMEM((HALO, SSM_CONV_DIM), F32)],
        compiler_params=_params(1),
        name="in_proj",
    )(x, gain.reshape(1, d), *pieces, conv_w, conv_b.reshape(1, -1), pad(dt_bias), pad(a_log))


def _sb_prepare(z, blocks):
    s = SB_SUB
    out = []
    for b, mask in blocks:
        zb = z[:, b * s:(b + 1) * s]
        tail = jnp.log(1.0 + jnp.exp(-jnp.abs(zb)))
        log_keep = -(jnp.maximum(zb, 0.0) + tail)
        if mask is not None:
            log_keep = jnp.where(mask, log_keep, 0.0)
        out.append((b, mask, jnp.minimum(zb, 0.0) - tail,
                    jnp.concatenate(_split_bf16(log_keep, 2), axis=1)))
    return out


def _sb_finish(prepared, sums, vb, total):
    s = SB_SUB
    weights = {}
    for (b, mask, log_beta, _), block_sums in zip(prepared, sums):
        w = jnp.exp(log_beta + block_sums[:, :s] + total)
        if mask is not None:
            w = jnp.where(mask, w, 0.0)
        weights[b] = w.astype(BF16)
        total = total + block_sums[:, s:]
    order = sorted(weights)
    w_all = jnp.concatenate([weights[b] for b in order], axis=1)
    return _dot(w_all, vb[order[0] * s:(order[-1] + 1) * s, :]), total


def _sb_kernel(q_ref, k_ref, v_ref, o_ref, acc_ref, tot_ref):
    i = pl.program_id(2)
    t, s = SB_TILE, SB_SUB
    rows = lax.broadcasted_iota(jnp.int32, (s, s), 0)
    cols = lax.broadcasted_iota(jnp.int32, (s, s), 1)
    later = (rows > cols).astype(BF16)
    half = jnp.concatenate([later, jnp.ones((s, s), BF16)], axis=1)
    sum_mat = jnp.concatenate([half, half], axis=0)
    before = cols < rows

    def visit(back, diagonal):
        work = []
        for a in range(SB_QTILES_PER_STEP):
            rs = slice(a * t, (a + 1) * t)
            j = i * SB_QTILES_PER_STEP + a - back
            start = pl.multiple_of(jnp.maximum(j, 0) * t, t)
            for h in range(SB_HEADS_PER_STEP):
                hs = slice(h * SB_HEAD_DIM, (h + 1) * SB_HEAD_DIM)
                z = lax.dot_general(q_ref[rs, hs], k_ref[pl.ds(start, t), hs],
                                    (((1,), (1,)), ((), ())), preferred_element_type=F32)
                if diagonal:
                    zero = jnp.zeros((s, s), F32)
                    units = [(_sb_prepare(z[:s], [(0, before)]), zero),
                             (_sb_prepare(z[s:], [(1, before), (0, None)]), zero)]
                else:
                    total = jnp.where(j >= 0, tot_ref[a, h], DEAD_TOTAL)
                    units = [(_sb_prepare(z, [(1, None), (0, None)]), total)]
                work.append((a, h, j, v_ref[pl.ds(start, t), hs], units))

        operands = [blk[3] for _, _, _, _, units in work for prep, _ in units for blk in prep]
        all_sums = _dot(jnp.concatenate(operands, axis=0), sum_mat)

        largest, row = None, 0
        for a, h, j, vb, units in work:
            parts, totals = [], []
            for prepared, total in units:
                sums = []
                for blk in prepared:
                    sums.append(all_sums[row:row + blk[3].shape[0], :])
                    row += blk[3].shape[0]
                part, total = _sb_finish(prepared, sums, vb, total)
                parts.append(part)
                totals.append(total)
            part = jnp.concatenate(parts, axis=0)
            total = jnp.concatenate(totals, axis=0)
            acc_ref[a, h] = part if diagonal else acc_ref[a, h] + part
            tot_ref[a, h] = total
            top = jnp.where(j >= 1, jnp.max(total), DEAD_TOTAL)
            largest = top if largest is None else jnp.maximum(largest, top)
        return largest

    def cond(carry):
        return carry[1] > LOG_WEIGHT_FLOOR

    def body(carry):
        return carry[0] + 1, visit(carry[0], False)

    lax.while_loop(cond, body, (jnp.int32(1), visit(0, True)))
    for a in range(SB_QTILES_PER_STEP):
        for h in range(SB_HEADS_PER_STEP):
            o_ref[a * t:(a + 1) * t, h * SB_HEAD_DIM:(h + 1) * SB_HEAD_DIM] = (
                acc_ref[a, h].astype(o_ref.dtype))


def _sb_attention(q, k, v, batch, seq):
    t = SB_TILE
    rows = SB_QTILES_PER_STEP * t
    nq = seq // rows
    width = SB_HEADS_PER_STEP * SB_HEAD_DIM
    qspec = pl.BlockSpec((rows, width), lambda b, g, i: (b * nq + i, g))
    kvspec = pl.BlockSpec((seq, width), lambda b, g, i: (b, g))
    chains = (SB_QTILES_PER_STEP, SB_HEADS_PER_STEP)
    return pl.pallas_call(
        _sb_kernel,
        out_shape=jax.ShapeDtypeStruct(q.shape, BF16),
        grid=(batch, SB_HEADS // SB_HEADS_PER_STEP, nq),
        in_specs=[qspec, kvspec, kvspec],
        out_specs=qspec,
        scratch_shapes=[pltpu.VMEM(chains + (t, SB_HEAD_DIM), F32),
                        pltpu.VMEM(chains + (t, SB_SUB), F32)],
        compiler_params=_params(3),
        name="sb_attention",
    )(q, k, v)


def _ssd_kernel(xc_ref, sc_ref, dsk_ref, o_ref, st_ref):
    n = SSM_CHUNK

    @pl.when(pl.program_id(1) == 0)
    def _():
        st_ref[...] = jnp.zeros_like(st_ref)

    rows = lax.broadcasted_iota(jnp.int32, (n, n), 0)
    cols = lax.broadcasted_iota(jnp.int32, (n, n), 1)
    causal = rows >= cols
    low = cols < SSM_HEAD_DIM
    low_row = low[0:1, :]
    b_off = SSM_D_INNER
    c_off = SSM_D_INNER + SSM_GROUPS * SSM_STATE

    def scalars(rs):
        names = ("cum", "cum_t", "ecum", "dt_t", "w_t")
        return {name: sc_ref[rs, k * LANES:(k + 1) * LANES] for k, name in enumerate(names)}

    chunks = [slice(c * n, (c + 1) * n) for c in range(SSM_CHUNKS_PER_STEP)]
    per_chunk = [scalars(rs) for rs in chunks]
    group_cache = {}

    def group(c, g):
        if (c, g) not in group_cache:
            bg = xc_ref[chunks[c], b_off + g * SSM_STATE:b_off + (g + 1) * SSM_STATE]
            cg = xc_ref[chunks[c], c_off + g * SSM_STATE:c_off + (g + 1) * SSM_STATE]
            cb = lax.dot_general(cg, bg, (((1,), (1,)), ((), ())), preferred_element_type=F32)
            group_cache[c, g] = (cb, cg.astype(F32), bg.astype(F32).T)
        return group_cache[c, g]

    for p in range(SSM_HEADS // 2):
        lanes = slice(p * LANES, (p + 1) * LANES)
        state = st_ref[p]
        for c, rs in enumerate(chunks):
            s = per_chunk[c]
            xs = xc_ref[rs, lanes].astype(F32)
            xs_pair = [jnp.where(low, xs, 0.0).astype(BF16), jnp.where(low, 0.0, xs).astype(BF16)]
            st_pair = [jnp.where(low, state, 0.0).astype(BF16),
                       jnp.where(low, 0.0, state).astype(BF16)]
            g_mats, c_mats, b_mats = [], [], []
            for h in (2 * p, 2 * p + 1):
                cb, cg, bg_t = group(c, h // SSM_HEADS_PER_GROUP)
                decay = jnp.where(
                    causal, jnp.exp(s["cum"][:, h:h + 1] - s["cum_t"][h:h + 1, :]), 0.0)
                g_mats.append((cb * decay * s["dt_t"][h:h + 1, :]).astype(BF16))
                c_mats.append((cg * s["ecum"][:, h:h + 1]).astype(BF16))
                b_mats.append((bg_t * s["w_t"][h:h + 1, :]).astype(BF16))
            y = _dot(jnp.concatenate(g_mats + c_mats, axis=1),
                     jnp.concatenate(xs_pair + st_pair, axis=0))
            o_ref[rs, lanes] = (y + dsk_ref[:, lanes] * xs).astype(o_ref.dtype)
            last = s["ecum"][n - 1:n, :]
            state_decay = jnp.where(low_row, last[:, 2 * p:2 * p + 1],
                                    last[:, 2 * p + 1:2 * p + 2])
            state = state * state_decay + _dot(jnp.concatenate(b_mats, axis=1),
                                               jnp.concatenate(xs_pair, axis=0))
        st_ref[p] = state


def _ssd(xc, scalars, d_skip, batch, seq):
    n = SSM_CHUNK * SSM_CHUNKS_PER_STEP
    nc = seq // n
    row = lambda w: pl.BlockSpec((n, w), lambda b, c: (b * nc + c, 0))
    return pl.pallas_call(
        _ssd_kernel,
        out_shape=jax.ShapeDtypeStruct((batch * seq, SSM_D_INNER), BF16),
        grid=(batch, nc),
        in_specs=[row(SSM_CONV_DIM), row(SCAN_SCALARS * LANES), _resident((1, SSM_D_INNER))],
        out_specs=row(SSM_D_INNER),
        scratch_shapes=[pltpu.VMEM((SSM_HEADS // 2, SSM_STATE, LANES), F32)],
        compiler_params=_params(2),
        name="ssd",
    )(xc, scalars, jnp.repeat(d_skip, SSM_HEAD_DIM).reshape(1, -1))


def _merge_kernel(x_ref, ysb_ref, y_ref, z_ref, gs_ref, gm_ref, nrm_ref, wsb_ref, wssm_ref,
                  wo_ref, o_ref, yn_ref):
    tm = MERGE_TOKEN_TILE
    low = lax.broadcasted_iota(jnp.int32, (tm, LANES), 1) < SSM_HEAD_DIM
    rsum = lambda v: jnp.sum(v, axis=-1, keepdims=True)
    y_sb = _dot(ysb_ref[...], wsb_ref[...])
    y_ssm = None
    pairs_per_part = SSM_GROUPS // 4
    for gp in range(SSM_GROUPS // 2):
        tiles = [slice((3 * gp + i) * LANES, (3 * gp + i + 1) * LANES) for i in range(3)]
        ys = [y_ref[:, s].astype(F32) * _silu(z_ref[:, s].astype(F32)) for s in tiles]
        sq = [v * v for v in ys]
        ss_a = rsum(sq[0]) + rsum(jnp.where(low, sq[1], 0.0))
        ss_b = rsum(sq[2]) + rsum(jnp.where(low, 0.0, sq[1]))
        r_a = lax.rsqrt(ss_a * (1.0 / SSM_GROUP_WIDTH) + NORM_EPS)
        r_b = lax.rsqrt(ss_b * (1.0 / SSM_GROUP_WIDTH) + NORM_EPS)
        scales = [r_a, jnp.where(low, r_a, r_b), r_b]
        for s, v, r in zip(tiles, ys, scales):
            yn_ref[:, s] = (v * r * nrm_ref[:, s]).astype(BF16)
        if (gp + 1) % pairs_per_part == 0:
            part = slice((gp + 1 - pairs_per_part) * 3 * LANES, (gp + 1) * 3 * LANES)
            term = _dot(yn_ref[:, part], wssm_ref[part, :])
            y_ssm = term if y_ssm is None else y_ssm + term
    merged = (jax.nn.sigmoid(gs_ref[...].astype(F32)) * y_sb
              + jax.nn.sigmoid(gm_ref[...].astype(F32)) * y_ssm)
    o_ref[...] = x_ref[...] + _dot(merged.astype(BF16), wo_ref[...])


def _merge(x, y_sb, y_scan, z, gate_sb, gate_ssm, ssm_norm, w_sb, w_ssm, w_out):
    t, d = x.shape
    tm = MERGE_TOKEN_TILE
    row = lambda w: pl.BlockSpec((tm, w), lambda i: (i, 0))
    return pl.pallas_call(
        _merge_kernel,
        out_shape=jax.ShapeDtypeStruct((t, d), F32),
        grid=(t // tm,),
        in_specs=[row(d), row(SB_WIDTH), row(SSM_D_INNER), row(SSM_D_INNER), row(d), row(d),
                  _resident((1, SSM_D_INNER)), _resident((SB_WIDTH, d)),
                  _resident((SSM_D_INNER, d)), _resident((d, d))],
        out_specs=row(d),
        scratch_shapes=[pltpu.VMEM((tm, SSM_D_INNER), BF16)],
        compiler_params=_params(1),
        name="merge",
    )(x, y_sb, y_scan, z, gate_sb, gate_ssm, ssm_norm.reshape(1, -1), w_sb.astype(BF16),
      w_ssm.astype(BF16), w_out.astype(BF16))


def kernel(x, ffn1_norm, ffn1_w_up, ffn1_w_down, mix_norm, w_in, conv_w, conv_b, dt_bias,
           a_log, d_skip, ssm_norm, w_branch_sb, w_branch_ssm, w_out, ffn2_norm, ffn2_w_up,
           ffn2_w_down, final_norm):
    batch, seq, d = x.shape
    depth = w_in.shape[0]
    assert d == D_MODEL and seq % (SB_TILE * SB_QTILES_PER_STEP) == 0
    assert seq % (SSM_CHUNK * SSM_CHUNKS_PER_STEP) == 0
    assert seq % PROJ_TOKEN_TILE == 0 and (batch * seq) % FFN_TOKEN_TILE == 0
    xf = x.reshape(batch * seq, d)
    for l in range(depth):
        xf = _ffn(xf, ffn1_norm[l], ffn1_w_up[l], ffn1_w_down[l], None)
        q, k, v, z, xc, scalars, gate_sb, gate_ssm = _in_proj(
            xf, mix_norm[l], w_in[l], conv_w[l], conv_b[l], dt_bias[l], a_log[l], seq)
        y_sb = _sb_attention(q, k, v, batch, seq)
        y_scan = _ssd(xc, scalars, d_skip[l], batch, seq)
        xf = _merge(xf, y_sb, y_scan, z, gate_sb, gate_ssm, ssm_norm[l], w_branch_sb[l],
                    w_branch_ssm[l], w_out[l])
        xf = _ffn(xf, ffn2_norm[l], ffn2_w_up[l], ffn2_w_down[l],
                  final_norm if l == depth - 1 else None)
    return xf.reshape(batch, seq, d)
```

```python
import functools

import jax
import jax.numpy as jnp
from jax import lax
from jax.experimental import pallas as pl
from jax.experimental.pallas import tpu as pltpu

F32 = jnp.float32
BF16 = jnp.bfloat16

D_MODEL = 1024
SB_HEADS = 4
SB_HEAD_DIM = 128
SB_WIDTH = SB_HEADS * SB_HEAD_DIM
SSM_D_INNER = 1536
SSM_HEAD_DIM = 64
SSM_HEADS = SSM_D_INNER // SSM_HEAD_DIM
SSM_GROUPS = 8
SSM_HEADS_PER_GROUP = SSM_HEADS // SSM_GROUPS
SSM_STATE = 128
SSM_CONV = 4
SSM_CONV_DIM = SSM_D_INNER + 2 * SSM_GROUPS * SSM_STATE
SSM_GROUP_WIDTH = SSM_D_INNER // SSM_GROUPS
FFN_HIDDEN = 2816
NORM_EPS = 1e-6

LANES = 128
SUBLANES = 8
MXU_DIM = 256
VMEM_LIMIT = 56 * 1024 * 1024

FFN_TOKEN_TILE = 512
FFN_HIDDEN_CHUNKS = (5 * MXU_DIM, 6 * MXU_DIM)
PROJ_TOKEN_TILE = 256
MERGE_TOKEN_TILE = 512
SB_TILE = 256
SB_SUB = 128
SB_HEADS_PER_STEP = 2
SB_QTILES_PER_STEP = 2
LOG_WEIGHT_FLOOR = -105.0
DEAD_TOTAL = -1e30
SSM_CHUNK = 128
SCAN_SCALARS = 5
CONV_COL_TILE = 256
HALO = SUBLANES

assert sum(FFN_HIDDEN_CHUNKS) == FFN_HIDDEN


def _params(n_axes, flags=None):
    return pltpu.CompilerParams(
        dimension_semantics=("arbitrary",) * n_axes, vmem_limit_bytes=VMEM_LIMIT, flags=flags)


def _resident(shape):
    zeros = (0,) * len(shape)
    return pl.BlockSpec(shape, lambda *_: zeros, pipeline_mode=pl.Buffered(1))


def _rms_norm(x, gain):
    ms = jnp.mean(x * x, axis=-1, keepdims=True)
    return x * lax.rsqrt(ms + NORM_EPS) * gain


def _silu(x):
    return x * jax.nn.sigmoid(x)


def _softplus(x):
    return jnp.maximum(x, 0.0) + jnp.log1p(jnp.exp(-jnp.abs(x)))


def _dot(a, b):
    return jnp.dot(a, b, preferred_element_type=F32)


def _split_bf16(x, parts):
    out = []
    for _ in range(parts - 1):
        hi = x.astype(BF16)
        out.append(hi)
        x = x - hi.astype(F32)
    out.append(x.astype(BF16))
    return out


def _ffn_kernel(x_ref, g_ref, wg_ref, wu_ref, wd_ref, *rest, final_norm):
    if final_norm:
        fg_ref, o_ref = rest
    else:
        (o_ref,) = rest
    x = x_ref[...]
    xn = _rms_norm(x, g_ref[...]).astype(BF16)
    acc, start = None, 0
    for width in FFN_HIDDEN_CHUNKS:
        cols = slice(start, start + width)
        start += width
        gate = _dot(xn, wg_ref[:, cols])
        up = _dot(xn, wu_ref[:, cols])
        act = (_silu(gate) * up).astype(BF16)
        part = _dot(act, wd_ref[cols, :])
        acc = part if acc is None else acc + part
    y = x + 0.5 * acc
    if final_norm:
        y = _rms_norm(y, fg_ref[...])
    o_ref[...] = y


def _ffn(x, gain, w_up, w_down, final_gain):
    t, d = x.shape
    tm = FFN_TOKEN_TILE
    w_gate = w_up[:, :FFN_HIDDEN].astype(BF16)
    w_lin = w_up[:, FFN_HIDDEN:].astype(BF16)
    w_dn = w_down.astype(BF16)
    row = pl.BlockSpec((tm, d), lambda i: (i, 0))
    in_specs = [row, _resident((1, d)), _resident((d, FFN_HIDDEN)),
                _resident((d, FFN_HIDDEN)), _resident((FFN_HIDDEN, d))]
    args = [x, gain.reshape(1, d), w_gate, w_lin, w_dn]
    if final_gain is not None:
        in_specs.append(_resident((1, d)))
        args.append(final_gain.reshape(1, d))
    return pl.pallas_call(
        functools.partial(_ffn_kernel, final_norm=final_gain is not None),
        out_shape=jax.ShapeDtypeStruct((t, d), F32),
        grid=(t // tm,),
        in_specs=in_specs,
        out_specs=row,
        compiler_params=_params(1),
        name="ffn",
    )(*args)


def _in_proj_kernel(x_ref, g_ref, wq_ref, wk_ref, wv_ref, wz_ref, wx_ref, wdt_ref,
                    wgs_ref, wgm_ref, convw_ref, convb_ref, dtb_ref, alog_ref, q_ref, k_ref,
                    v_ref, z_ref, xc_ref, sc_ref, gs_ref, gm_ref, xpad_ref, *, tiles_per_seq):
    tm = PROJ_TOKEN_TILE
    n = SSM_CHUNK

    @pl.when(pl.program_id(0) == 0)
    def _():
        xpad_ref[...] = jnp.zeros_like(xpad_ref)

    xn = _rms_norm(x_ref[...], g_ref[...]).astype(BF16)
    starts_seq = lax.rem(pl.program_id(0), tiles_per_seq) == 0

    def project(w_ref, o_ref, scale=None):
        def tile(cs):
            y = _dot(xn, w_ref[:, cs])
            o_ref[:, cs] = (y if scale is None else y * scale).astype(o_ref.dtype)
        width = o_ref.shape[1]
        step = min(width, CONV_COL_TILE)
        return [functools.partial(tile, slice(c, c + step)) for c in range(0, width, step)]

    def scan_scalars():
        dt_all = _softplus(_dot(xn, wdt_ref[...]) + dtb_ref[...])
        neg_a = -jnp.exp(alog_ref[...])
        incl = (lax.broadcasted_iota(jnp.int32, (n, n), 0)
                >= lax.broadcasted_iota(jnp.int32, (n, n), 1)).astype(BF16)
        incl3 = jnp.concatenate([incl] * 3, axis=1)
        for c in range(tm // n):
            rs = slice(c * n, (c + 1) * n)
            dt = dt_all[rs, :]
            cum = _dot(incl3, jnp.concatenate(_split_bf16(dt * neg_a, 3), axis=0))
            parts = [cum, cum.T, jnp.exp(cum), dt.T, (dt * jnp.exp(cum[n - 1:n, :] - cum)).T]
            for k, part in enumerate(parts):
                sc_ref[rs, k * LANES:(k + 1) * LANES] = part

    plain = ([scan_scalars] + project(wq_ref, q_ref, SB_HEAD_DIM ** -0.5)
             + project(wk_ref, k_ref) + project(wv_ref, v_ref) + project(wz_ref, z_ref)
             + project(wgs_ref, gs_ref) + project(wgm_ref, gm_ref))

    def conv_tile(cs, y, tail):
        ext = jnp.concatenate([tail, y], axis=0)
        acc = convb_ref[:, cs]
        for i in range(SSM_CONV - 1):
            shifted = pltpu.roll(ext, SSM_CONV - 1 - i, axis=0)[HALO:, :]
            acc = acc + convw_ref[i:i + 1, cs] * shifted
        acc = acc + convw_ref[SSM_CONV - 1:SSM_CONV, cs] * y
        xc_ref[:, cs] = _silu(acc).astype(BF16)

    n_tiles = SSM_CONV_DIM // CONV_COL_TILE
    pending = None
    for ct in range(n_tiles):
        cs = slice(ct * CONV_COL_TILE, (ct + 1) * CONV_COL_TILE)
        y = _dot(xn, wx_ref[:, cs])
        tail = jnp.where(starts_seq, 0.0, xpad_ref[:, cs])
        xpad_ref[:, cs] = y[tm - HALO:tm, :]
        for task in plain[ct::n_tiles]:
            task()
        if pending is not None:
            conv_tile(*pending)
        pending = (cs, y, tail)
    conv_tile(*pending)


def _in_proj(x, gain, w_in, conv_w, conv_b, dt_bias, a_log, seq):
    t, d = x.shape
    tm = PROJ_TOKEN_TILE
    pad = lambda v: jnp.pad(v, (0, LANES - SSM_HEADS)).reshape(1, LANES)
    widths = (SB_WIDTH, SB_WIDTH, SB_WIDTH, SSM_D_INNER, SSM_CONV_DIM, SSM_HEADS,
              D_MODEL, D_MODEL)
    pieces, start = [], 0
    for w in widths:
        pieces.append(w_in[:, start:start + w].astype(BF16))
        start += w
    pieces[5] = jnp.pad(pieces[5], ((0, 0), (0, LANES - SSM_HEADS)))
    out_widths = (SB_WIDTH, SB_WIDTH, SB_WIDTH, SSM_D_INNER, SSM_CONV_DIM, SCAN_SCALARS * LANES,
                  D_MODEL, D_MODEL)
    out_dtypes = (BF16, BF16, BF16, BF16, BF16, F32, BF16, BF16)
    row = lambda w: pl.BlockSpec((tm, w), lambda i: (i, 0))
    return pl.pallas_call(
        functools.partial(_in_proj_kernel, tiles_per_seq=seq // tm),
        out_shape=[jax.ShapeDtypeStruct((t, w), dt) for w, dt in zip(out_widths, out_dtypes)],
        grid=(t // tm,),
        in_specs=([row(d), _resident((1, d))] + [_resident(p.shape) for p in pieces]
                  + [_resident((SSM_CONV, SSM_CONV_DIM)), _resident((1, SSM_CONV_DIM)),
                     _resident((1, LANES)), _resident((1, LANES))]),
        out_specs=[row(w) for w in out_widths],
        scratch_shapes=[pltpu.VMEM((HALO, SSM_CONV_DIM), F32)],
        compiler_params=_params(1),
        name="in_proj",
    )(x, gain.reshape(1, d), *pieces, conv_w, conv_b.reshape(1, -1), pad(dt_bias), pad(a_log))


def _sb_prepare(z, blocks):
    s = SB_SUB
    out = []
    for b, mask in blocks:
        zb = z[:, b * s:(b + 1) * s]
        tail = jnp.log(1.0 + jnp.exp(-jnp.abs(zb)))
        log_keep = -(jnp.maximum(zb, 0.0) + tail)
        if mask is not None:
            log_keep = jnp.where(mask, log_keep, 0.0)
        out.append((b, mask, jnp.minimum(zb, 0.0) - tail,
                    jnp.concatenate(_split_bf16(log_keep, 2), axis=1)))
    return out


def _sb_finish(prepared, sums, vb, total):
    s = SB_SUB
    weights = {}
    for (b, mask, log_beta, _), block_sums in zip(prepared, sums):
        w = jnp.exp(log_beta + block_sums[:, :s] + total)
        if mask is not None:
            w = jnp.where(mask, w, 0.0)
        weights[b] = w.astype(BF16)
        total = total + block_sums[:, s:]
    order = sorted(weights)
    w_all = jnp.concatenate([weights[b] for b in order], axis=1)
    return _dot(w_all, vb[order[0] * s:(order[-1] + 1) * s, :]), total


def _sb_kernel(q_ref, k_ref, v_ref, o_ref, acc_ref, tot_ref):
    i = pl.program_id(2)
    t, s = SB_TILE, SB_SUB
    rows = lax.broadcasted_iota(jnp.int32, (s, s), 0)
    cols = lax.broadcasted_iota(jnp.int32, (s, s), 1)
    later = (rows > cols).astype(BF16)
    half = jnp.concatenate([later, jnp.ones((s, s), BF16)], axis=1)
    sum_mat = jnp.concatenate([half, half], axis=0)
    before = cols < rows

    def visit(back, diagonal):
        work = []
        for a in range(SB_QTILES_PER_STEP):
            rs = slice(a * t, (a + 1) * t)
            j = i * SB_QTILES_PER_STEP + a - back
            start = pl.multiple_of(jnp.maximum(j, 0) * t, t)
            for h in range(SB_HEADS_PER_STEP):
                hs = slice(h * SB_HEAD_DIM, (h + 1) * SB_HEAD_DIM)
                z = lax.dot_general(q_ref[rs, hs], k_ref[pl.ds(start, t), hs],
                                    (((1,), (1,)), ((), ())), preferred_element_type=F32)
                if diagonal:
                    zero = jnp.zeros((s, s), F32)
                    units = [(_sb_prepare(z[:s], [(0, before)]), zero),
                             (_sb_prepare(z[s:], [(1, before), (0, None)]), zero)]
                else:
                    total = jnp.where(j >= 0, tot_ref[a, h], DEAD_TOTAL)
                    units = [(_sb_prepare(z, [(1, None), (0, None)]), total)]
                work.append((a, h, j, v_ref[pl.ds(start, t), hs], units))

        operands = [blk[3] for _, _, _, _, units in work for prep, _ in units for blk in prep]
        all_sums = _dot(jnp.concatenate(operands, axis=0), sum_mat)

        largest, row = None, 0
        for a, h, j, vb, units in work:
            parts, totals = [], []
            for prepared, total in units:
                sums = []
                for blk in prepared:
                    sums.append(all_sums[row:row + blk[3].shape[0], :])
                    row += blk[3].shape[0]
                part, total = _sb_finish(prepared, sums, vb, total)
                parts.append(part)
                totals.append(total)
            part = jnp.concatenate(parts, axis=0)
            total = jnp.concatenate(totals, axis=0)
            acc_ref[a, h] = part if diagonal else acc_ref[a, h] + part
            tot_ref[a, h] = total
            top = jnp.where(j >= 1, jnp.max(total), DEAD_TOTAL)
            largest = top if largest is None else jnp.maximum(largest, top)
        return largest

    def cond(carry):
        return carry[1] > LOG_WEIGHT_FLOOR

    def body(carry):
        return carry[0] + 1, visit(carry[0], False)

    lax.while_loop(cond, body, (jnp.int32(1), visit(0, True)))
    for a in range(SB_QTILES_PER_STEP):
        for h in range(SB_HEADS_PER_STEP):
            o_ref[a * t:(a + 1) * t, h * SB_HEAD_DIM:(h + 1) * SB_HEAD_DIM] = (
                acc_ref[a, h].astype(o_ref.dtype))


def _sb_attention(q, k, v, batch, seq):
    t = SB_TILE
    rows = SB_QTILES_PER_STEP * t
    nq = seq // rows
    width = SB_HEADS_PER_STEP * SB_HEAD_DIM
    qspec = pl.BlockSpec((rows, width), lambda b, g, i: (b * nq + i, g))
    kvspec = pl.BlockSpec((seq, width), lambda b, g, i: (b, g))
    chains = (SB_QTILES_PER_STEP, SB_HEADS_PER_STEP)
    return pl.pallas_call(
        _sb_kernel,
        out_shape=jax.ShapeDtypeStruct(q.shape, BF16),
        grid=(batch, SB_HEADS // SB_HEADS_PER_STEP, nq),
        in_specs=[qspec, kvspec, kvspec],
        out_specs=qspec,
        scratch_shapes=[pltpu.VMEM(chains + (t, SB_HEAD_DIM), F32),
                        pltpu.VMEM(chains + (t, SB_SUB), F32)],
        compiler_params=_params(3),
        name="sb_attention",
    )(q, k, v)


def _ssd_kernel(xc_ref, sc_ref, dsk_ref, o_ref, st_ref):
    n = SSM_CHUNK
    n_seq = xc_ref.shape[0]

    @pl.when(pl.program_id(0) == 0)
    def _():
        st_ref[...] = jnp.zeros_like(st_ref)

    rows = lax.broadcasted_iota(jnp.int32, (n, n), 0)
    cols = lax.broadcasted_iota(jnp.int32, (n, n), 1)
    causal = rows >= cols
    low = cols < SSM_HEAD_DIM
    low_row = low[0:1, :]
    b_off = SSM_D_INNER
    c_off = SSM_D_INNER + SSM_GROUPS * SSM_STATE

    def scalars(b):
        names = ("cum", "cum_t", "ecum", "dt_t", "w_t")
        return {name: sc_ref[b, :, k * LANES:(k + 1) * LANES] for k, name in enumerate(names)}

    per_seq = [scalars(b) for b in range(n_seq)]
    group_cache = {}

    def group(b, g):
        if (b, g) not in group_cache:
            bg = xc_ref[b, :, b_off + g * SSM_STATE:b_off + (g + 1) * SSM_STATE]
            cg = xc_ref[b, :, c_off + g * SSM_STATE:c_off + (g + 1) * SSM_STATE]
            cb = lax.dot_general(cg, bg, (((1,), (1,)), ((), ())), preferred_element_type=F32)
            group_cache[b, g] = (cb, cg.astype(F32), bg.astype(F32).T)
        return group_cache[b, g]

    for p in range(SSM_HEADS // 2):
        lanes = slice(p * LANES, (p + 1) * LANES)
        for b in range(n_seq):
            s = per_seq[b]
            state = st_ref[b, p]
            xs = xc_ref[b, :, lanes].astype(F32)
            xs_pair = [jnp.where(low, xs, 0.0).astype(BF16), jnp.where(low, 0.0, xs).astype(BF16)]
            st_pair = [jnp.where(low, state, 0.0).astype(BF16),
                       jnp.where(low, 0.0, state).astype(BF16)]
            g_mats, c_mats, b_mats = [], [], []
            for h in (2 * p, 2 * p + 1):
                cb, cg, bg_t = group(b, h // SSM_HEADS_PER_GROUP)
                decay = jnp.where(
                    causal, jnp.exp(s["cum"][:, h:h + 1] - s["cum_t"][h:h + 1, :]), 0.0)
                g_mats.append((cb * decay * s["dt_t"][h:h + 1, :]).astype(BF16))
                c_mats.append((cg * s["ecum"][:, h:h + 1]).astype(BF16))
                b_mats.append((bg_t * s["w_t"][h:h + 1, :]).astype(BF16))
            y = _dot(jnp.concatenate(g_mats + c_mats, axis=1),
                     jnp.concatenate(xs_pair + st_pair, axis=0))
            o_ref[b, :, lanes] = (y + dsk_ref[:, lanes] * xs).astype(o_ref.dtype)
            last = s["ecum"][n - 1:n, :]
            state_decay = jnp.where(low_row, last[:, 2 * p:2 * p + 1],
                                    last[:, 2 * p + 1:2 * p + 2])
            st_ref[b, p] = state * state_decay + _dot(jnp.concatenate(b_mats, axis=1),
                                                      jnp.concatenate(xs_pair, axis=0))


def _ssd(xc, scalars, d_skip, batch, seq):
    n = SSM_CHUNK
    block = lambda w: pl.BlockSpec((batch, n, w), lambda c: (0, c, 0))
    return pl.pallas_call(
        _ssd_kernel,
        out_shape=jax.ShapeDtypeStruct((batch, seq, SSM_D_INNER), BF16),
        grid=(seq // n,),
        in_specs=[block(SSM_CONV_DIM), block(SCAN_SCALARS * LANES), _resident((1, SSM_D_INNER))],
        out_specs=block(SSM_D_INNER),
        scratch_shapes=[pltpu.VMEM((batch, SSM_HEADS // 2, SSM_STATE, LANES), F32)],
        compiler_params=_params(1),
        name="ssd",
    )(xc.reshape(batch, seq, -1), scalars.reshape(batch, seq, -1),
      jnp.repeat(d_skip, SSM_HEAD_DIM).reshape(1, -1)).reshape(batch * seq, SSM_D_INNER)


def _merge_kernel(x_ref, ysb_ref, y_ref, z_ref, gs_ref, gm_ref, nrm_ref, wsb_ref, wssm_ref,
                  wo_ref, o_ref, yn_ref):
    tm = MERGE_TOKEN_TILE
    low = lax.broadcasted_iota(jnp.int32, (tm, LANES), 1) < SSM_HEAD_DIM
    rsum = lambda v: jnp.sum(v, axis=-1, keepdims=True)
    y_sb = _dot(ysb_ref[...], wsb_ref[...])
    y_ssm = None
    pairs_per_part = SSM_GROUPS // 4
    for gp in range(SSM_GROUPS // 2):
        tiles = [slice((3 * gp + i) * LANES, (3 * gp + i + 1) * LANES) for i in range(3)]
        ys = [y_ref[:, s].astype(F32) * _silu(z_ref[:, s].astype(F32)) for s in tiles]
        sq = [v * v for v in ys]
        ss_a = rsum(sq[0]) + rsum(jnp.where(low, sq[1], 0.0))
        ss_b = rsum(sq[2]) + rsum(jnp.where(low, 0.0, sq[1]))
        r_a = lax.rsqrt(ss_a * (1.0 / SSM_GROUP_WIDTH) + NORM_EPS)
        r_b = lax.rsqrt(ss_b * (1.0 / SSM_GROUP_WIDTH) + NORM_EPS)
        scales = [r_a, jnp.where(low, r_a, r_b), r_b]
        for s, v, r in zip(tiles, ys, scales):
            yn_ref[:, s] = (v * r * nrm_ref[:, s]).astype(BF16)
        if (gp + 1) % pairs_per_part == 0:
            part = slice((gp + 1 - pairs_per_part) * 3 * LANES, (gp + 1) * 3 * LANES)
            term = _dot(yn_ref[:, part], wssm_ref[part, :])
            y_ssm = term if y_ssm is None else y_ssm + term
    merged = (jax.nn.sigmoid(gs_ref[...].astype(F32)) * y_sb
              + jax.nn.sigmoid(gm_ref[...].astype(F32)) * y_ssm)
    o_ref[...] = x_ref[...] + _dot(merged.astype(BF16), wo_ref[...])


def _merge(x, y_sb, y_scan, z, gate_sb, gate_ssm, ssm_norm, w_sb, w_ssm, w_out):
    t, d = x.shape
    tm = MERGE_TOKEN_TILE
    row = lambda w: pl.BlockSpec((tm, w), lambda i: (i, 0))
    return pl.pallas_call(
        _merge_kernel,
        out_shape=jax.ShapeDtypeStruct((t, d), F32),
        grid=(t // tm,),
        in_specs=[row(d), row(SB_WIDTH), row(SSM_D_INNER), row(SSM_D_INNER), row(d), row(d),
                  _resident((1, SSM_D_INNER)), _resident((SB_WIDTH, d)),
                  _resident((SSM_D_INNER, d)), _resident((d, d))],
        out_specs=row(d),
        scratch_shapes=[pltpu.VMEM((tm, SSM_D_INNER), BF16)],
        compiler_params=_params(1),
        name="merge",
    )(x, y_sb, y_scan, z, gate_sb, gate_ssm, ssm_norm.reshape(1, -1), w_sb.astype(BF16),
      w_ssm.astype(BF16), w_out.astype(BF16))


def kernel(x, ffn1_norm, ffn1_w_up, ffn1_w_down, mix_norm, w_in, conv_w, conv_b, dt_bias,
           a_log, d_skip, ssm_norm, w_branch_sb, w_branch_ssm, w_out, ffn2_norm, ffn2_w_up,
           ffn2_w_down, final_norm):
    batch, seq, d = x.shape
    depth = w_in.shape[0]
    assert d == D_MODEL and seq % (SB_TILE * SB_QTILES_PER_STEP) == 0
    assert seq % SSM_CHUNK == 0
    assert seq % PROJ_TOKEN_TILE == 0 and (batch * seq) % FFN_TOKEN_TILE == 0
    xf = x.reshape(batch * seq, d)
    for l in range(depth):
        xf = _ffn(xf, ffn1_norm[l], ffn1_w_up[l], ffn1_w_down[l], None)
        q, k, v, z, xc, scalars, gate_sb, gate_ssm = _in_proj(
            xf, mix_norm[l], w_in[l], conv_w[l], conv_b[l], dt_bias[l], a_log[l], seq)
        y_sb = _sb_attention(q, k, v, batch, seq)
        y_scan = _ssd(xc, scalars, d_skip[l], batch, seq)
        xf = _merge(xf, y_sb, y_scan, z, gate_sb, gate_ssm, ssm_norm[l], w_branch_sb[l],
                    w_branch_ssm[l], w_out[l])
        xf = _ffn(xf, ffn2_norm[l], ffn2_w_up[l], ffn2_w_down[l],
                  final_norm if l == depth - 1 else None)
    return xf.reshape(batch, seq, d)
```

```python
import functools

import jax
import jax.numpy as jnp
from jax import lax
from jax.experimental import pallas as pl
from jax.experimental.pallas import tpu as pltpu

F32 = jnp.float32
BF16 = jnp.bfloat16

D_MODEL = 1024
SB_HEADS = 4
SB_HEAD_DIM = 128
SB_WIDTH = SB_HEADS * SB_HEAD_DIM
SSM_D_INNER = 1536
SSM_HEAD_DIM = 64
SSM_HEADS = SSM_D_INNER // SSM_HEAD_DIM
SSM_GROUPS = 8
SSM_HEADS_PER_GROUP = SSM_HEADS // SSM_GROUPS
SSM_STATE = 128
SSM_CONV = 4
SSM_CONV_DIM = SSM_D_INNER + 2 * SSM_GROUPS * SSM_STATE
SSM_GROUP_WIDTH = SSM_D_INNER // SSM_GROUPS
FFN_HIDDEN = 2816
NORM_EPS = 1e-6

LANES = 128
SUBLANES = 8
MXU_DIM = 256
VMEM_LIMIT = 56 * 1024 * 1024

FFN_TOKEN_TILE = 512
FFN_HIDDEN_CHUNKS = (5 * MXU_DIM, 6 * MXU_DIM)
PROJ_TOKEN_TILE = 512
MERGE_TOKEN_TILE = 512
SB_TILE = 256
SB_SUB = 128
SB_HEADS_PER_STEP = 2
SB_QTILES_PER_STEP = 2
LOG_WEIGHT_FLOOR = -105.0
DEAD_TOTAL = -1e30
SSM_CHUNK = 128
SCAN_SCALARS = 5
CONV_COL_TILE = 256
HALO = SUBLANES

assert sum(FFN_HIDDEN_CHUNKS) == FFN_HIDDEN


def _params(n_axes, flags=None):
    return pltpu.CompilerParams(
        dimension_semantics=("arbitrary",) * n_axes, vmem_limit_bytes=VMEM_LIMIT, flags=flags)


def _resident(shape):
    zeros = (0,) * len(shape)
    return pl.BlockSpec(shape, lambda *_: zeros, pipeline_mode=pl.Buffered(1))


def _rms_norm(x, gain):
    ms = jnp.mean(x * x, axis=-1, keepdims=True)
    return x * lax.rsqrt(ms + NORM_EPS) * gain


def _silu(x):
    return x * jax.nn.sigmoid(x)


def _softplus(x):
    return jnp.maximum(x, 0.0) + jnp.log1p(jnp.exp(-jnp.abs(x)))


def _dot(a, b):
    return jnp.dot(a, b, preferred_element_type=F32)


def _split_bf16(x, parts):
    out = []
    for _ in range(parts - 1):
        hi = x.astype(BF16)
        out.append(hi)
        x = x - hi.astype(F32)
    out.append(x.astype(BF16))
    return out


def _ffn_kernel(x_ref, g_ref, wg_ref, wu_ref, wd_ref, *rest, final_norm):
    if final_norm:
        fg_ref, o_ref = rest
    else:
        (o_ref,) = rest
    x = x_ref[...]
    xn = _rms_norm(x, g_ref[...]).astype(BF16)
    acc, start = None, 0
    for width in FFN_HIDDEN_CHUNKS:
        cols = slice(start, start + width)
        start += width
        gate = _dot(xn, wg_ref[:, cols])
        up = _dot(xn, wu_ref[:, cols])
        act = (_silu(gate) * up).astype(BF16)
        part = _dot(act, wd_ref[cols, :])
        acc = part if acc is None else acc + part
    y = x + 0.5 * acc
    if final_norm:
        y = _rms_norm(y, fg_ref[...])
    o_ref[...] = y


def _ffn(x, gain, w_up, w_down, final_gain):
    t, d = x.shape
    tm = FFN_TOKEN_TILE
    w_gate = w_up[:, :FFN_HIDDEN].astype(BF16)
    w_lin = w_up[:, FFN_HIDDEN:].astype(BF16)
    w_dn = w_down.astype(BF16)
    row = pl.BlockSpec((tm, d), lambda i: (i, 0))
    in_specs = [row, _resident((1, d)), _resident((d, FFN_HIDDEN)),
                _resident((d, FFN_HIDDEN)), _resident((FFN_HIDDEN, d))]
    args = [x, gain.reshape(1, d), w_gate, w_lin, w_dn]
    if final_gain is not None:
        in_specs.append(_resident((1, d)))
        args.append(final_gain.reshape(1, d))
    return pl.pallas_call(
        functools.partial(_ffn_kernel, final_norm=final_gain is not None),
        out_shape=jax.ShapeDtypeStruct((t, d), F32),
        grid=(t // tm,),
        in_specs=in_specs,
        out_specs=row,
        compiler_params=_params(1),
        name="ffn",
    )(*args)


def _in_proj_kernel(x_ref, g_ref, wq_ref, wk_ref, wv_ref, wz_ref, wx_ref, wdt_ref,
                    wgs_ref, wgm_ref, convw_ref, convb_ref, dtb_ref, alog_ref, qkv_ref, z_ref,
                    xc_ref, sc_ref, gates_ref, xpad_ref, *, tiles_per_seq):
    tm = PROJ_TOKEN_TILE
    n = SSM_CHUNK

    @pl.when(pl.program_id(0) == 0)
    def _():
        xpad_ref[...] = jnp.zeros_like(xpad_ref)

    xn = _rms_norm(x_ref[...], g_ref[...]).astype(BF16)
    starts_seq = lax.rem(pl.program_id(0), tiles_per_seq) == 0

    def project(w_ref, o_ref, offset=0, scale=None):
        def tile(c):
            y = _dot(xn, w_ref[:, c:c + CONV_COL_TILE])
            o_ref[:, offset + c:offset + c + CONV_COL_TILE] = (
                y if scale is None else y * scale).astype(o_ref.dtype)
        return [functools.partial(tile, c) for c in range(0, w_ref.shape[1], CONV_COL_TILE)]

    def scan_scalars():
        dt_all = _softplus(_dot(xn, wdt_ref[...]) + dtb_ref[...])
        neg_a = -jnp.exp(alog_ref[...])
        incl = (lax.broadcasted_iota(jnp.int32, (n, n), 0)
                >= lax.broadcasted_iota(jnp.int32, (n, n), 1)).astype(BF16)
        incl3 = jnp.concatenate([incl] * 3, axis=1)
        for c in range(tm // n):
            rs = slice(c * n, (c + 1) * n)
            dt = dt_all[rs, :]
            cum = _dot(incl3, jnp.concatenate(_split_bf16(dt * neg_a, 3), axis=0))
            parts = [cum, cum.T, jnp.exp(cum), dt.T, (dt * jnp.exp(cum[n - 1:n, :] - cum)).T]
            for k, part in enumerate(parts):
                sc_ref[rs, k * LANES:(k + 1) * LANES] = part

    plain = ([scan_scalars] + project(wq_ref, qkv_ref, 0, SB_HEAD_DIM ** -0.5)
             + project(wk_ref, qkv_ref, SB_WIDTH) + project(wv_ref, qkv_ref, 2 * SB_WIDTH)
             + project(wz_ref, z_ref) + project(wgs_ref, gates_ref)
             + project(wgm_ref, gates_ref, D_MODEL))

    def conv_tile(cs, y, tail):
        ext = jnp.concatenate([tail, y], axis=0)
        acc = convb_ref[:, cs]
        for i in range(SSM_CONV - 1):
            shifted = pltpu.roll(ext, SSM_CONV - 1 - i, axis=0)[HALO:, :]
            acc = acc + convw_ref[i:i + 1, cs] * shifted
        acc = acc + convw_ref[SSM_CONV - 1:SSM_CONV, cs] * y
        xc_ref[:, cs] = _silu(acc).astype(BF16)

    n_tiles = SSM_CONV_DIM // CONV_COL_TILE
    pending = None
    for ct in range(n_tiles):
        cs = slice(ct * CONV_COL_TILE, (ct + 1) * CONV_COL_TILE)
        y = _dot(xn, wx_ref[:, cs])
        tail = jnp.where(starts_seq, 0.0, xpad_ref[:, cs])
        xpad_ref[:, cs] = y[tm - HALO:tm, :]
        for task in plain[ct::n_tiles]:
            task()
        if pending is not None:
            conv_tile(*pending)
        pending = (cs, y, tail)
    conv_tile(*pending)


def _in_proj(x, gain, w_in, conv_w, conv_b, dt_bias, a_log, seq):
    t, d = x.shape
    tm = PROJ_TOKEN_TILE
    pad = lambda v: jnp.pad(v, (0, LANES - SSM_HEADS)).reshape(1, LANES)
    widths = (SB_WIDTH, SB_WIDTH, SB_WIDTH, SSM_D_INNER, SSM_CONV_DIM, SSM_HEADS,
              D_MODEL, D_MODEL)
    pieces, start = [], 0
    for w in widths:
        pieces.append(w_in[:, start:start + w].astype(BF16))
        start += w
    pieces[5] = jnp.pad(pieces[5], ((0, 0), (0, LANES - SSM_HEADS)))
    out_widths = (3 * SB_WIDTH, SSM_D_INNER, SSM_CONV_DIM, SCAN_SCALARS * LANES, 2 * D_MODEL)
    out_dtypes = (BF16, BF16, BF16, F32, BF16)
    row = lambda w: pl.BlockSpec((tm, w), lambda i: (i, 0))
    return pl.pallas_call(
        functools.partial(_in_proj_kernel, tiles_per_seq=seq // tm),
        out_shape=[jax.ShapeDtypeStruct((t, w), dt) for w, dt in zip(out_widths, out_dtypes)],
        grid=(t // tm,),
        in_specs=([row(d), _resident((1, d))] + [_resident(p.shape) for p in pieces]
                  + [_resident((SSM_CONV, SSM_CONV_DIM)), _resident((1, SSM_CONV_DIM)),
                     _resident((1, LANES)), _resident((1, LANES))]),
        out_specs=[row(w) for w in out_widths],
        scratch_shapes=[pltpu.VMEM((HALO, SSM_CONV_DIM), F32)],
        compiler_params=_params(1),
        name="in_proj",
    )(x, gain.reshape(1, d), *pieces, conv_w, conv_b.reshape(1, -1), pad(dt_bias), pad(a_log))


def _sb_prepare(z, blocks):
    s = SB_SUB
    out = []
    for b, mask in blocks:
        zb = z[:, b * s:(b + 1) * s]
        tail = jnp.log(1.0 + jnp.exp(-jnp.abs(zb)))
        log_keep = -(jnp.maximum(zb, 0.0) + tail)
        if mask is not None:
            log_keep = jnp.where(mask, log_keep, 0.0)
        out.append((b, mask, jnp.minimum(zb, 0.0) - tail,
                    jnp.concatenate(_split_bf16(log_keep, 2), axis=1)))
    return out


def _sb_finish(prepared, sums, vb, total):
    s = SB_SUB
    weights = {}
    for (b, mask, log_beta, _), block_sums in zip(prepared, sums):
        w = jnp.exp(log_beta + block_sums[:, :s] + total)
        if mask is not None:
            w = jnp.where(mask, w, 0.0)
        weights[b] = w.astype(BF16)
        total = total + block_sums[:, s:]
    order = sorted(weights)
    w_all = jnp.concatenate([weights[b] for b in order], axis=1)
    return _dot(w_all, vb[order[0] * s:(order[-1] + 1) * s, :]), total


def _sb_kernel(q_ref, k_ref, v_ref, o_ref, acc_ref, tot_ref):
    i = pl.program_id(2)
    t, s = SB_TILE, SB_SUB
    rows = lax.broadcasted_iota(jnp.int32, (s, s), 0)
    cols = lax.broadcasted_iota(jnp.int32, (s, s), 1)
    later = (rows > cols).astype(BF16)
    half = jnp.concatenate([later, jnp.ones((s, s), BF16)], axis=1)
    sum_mat = jnp.concatenate([half, half], axis=0)
    before = cols < rows

    def visit(back, diagonal):
        work = []
        for a in range(SB_QTILES_PER_STEP):
            rs = slice(a * t, (a + 1) * t)
            j = i * SB_QTILES_PER_STEP + a - back
            start = pl.multiple_of(jnp.maximum(j, 0) * t, t)
            for h in range(SB_HEADS_PER_STEP):
                hs = slice(h * SB_HEAD_DIM, (h + 1) * SB_HEAD_DIM)
                z = lax.dot_general(q_ref[rs, hs], k_ref[pl.ds(start, t), hs],
                                    (((1,), (1,)), ((), ())), preferred_element_type=F32)
                if diagonal:
                    zero = jnp.zeros((s, s), F32)
                    units = [(_sb_prepare(z[:s], [(0, before)]), zero),
                             (_sb_prepare(z[s:], [(1, before), (0, None)]), zero)]
                else:
                    total = jnp.where(j >= 0, tot_ref[a, h], DEAD_TOTAL)
                    units = [(_sb_prepare(z, [(1, None), (0, None)]), total)]
                work.append((a, h, j, v_ref[pl.ds(start, t), hs], units))

        operands = [blk[3] for _, _, _, _, units in work for prep, _ in units for blk in prep]
        all_sums = _dot(jnp.concatenate(operands, axis=0), sum_mat)

        largest, row = None, 0
        for a, h, j, vb, units in work:
            parts, totals = [], []
            for prepared, total in units:
                sums = []
                for blk in prepared:
                    sums.append(all_sums[row:row + blk[3].shape[0], :])
                    row += blk[3].shape[0]
                part, total = _sb_finish(prepared, sums, vb, total)
                parts.append(part)
                totals.append(total)
            part = jnp.concatenate(parts, axis=0)
            total = jnp.concatenate(totals, axis=0)
            acc_ref[a, h] = part if diagonal else acc_ref[a, h] + part
            tot_ref[a, h] = total
            top = jnp.where(j >= 1, jnp.max(total), DEAD_TOTAL)
            largest = top if largest is None else jnp.maximum(largest, top)
        return largest

    def cond(carry):
        return carry[1] > LOG_WEIGHT_FLOOR

    def body(carry):
        return carry[0] + 1, visit(carry[0], False)

    lax.while_loop(cond, body, (jnp.int32(1), visit(0, True)))
    for a in range(SB_QTILES_PER_STEP):
        for h in range(SB_HEADS_PER_STEP):
            o_ref[a * t:(a + 1) * t, h * SB_HEAD_DIM:(h + 1) * SB_HEAD_DIM] = (
                acc_ref[a, h].astype(o_ref.dtype))


def _sb_attention(qkv, batch, seq):
    t = SB_TILE
    rows = SB_QTILES_PER_STEP * t
    nq = seq // rows
    width = SB_HEADS_PER_STEP * SB_HEAD_DIM
    groups = SB_HEADS // SB_HEADS_PER_STEP
    qspec = pl.BlockSpec((rows, width), lambda b, g, i: (b * nq + i, g))
    kspec = pl.BlockSpec((seq, width), lambda b, g, i: (b, groups + g))
    vspec = pl.BlockSpec((seq, width), lambda b, g, i: (b, 2 * groups + g))
    chains = (SB_QTILES_PER_STEP, SB_HEADS_PER_STEP)
    return pl.pallas_call(
        _sb_kernel,
        out_shape=jax.ShapeDtypeStruct((batch * seq, SB_WIDTH), BF16),
        grid=(batch, groups, nq),
        in_specs=[qspec, kspec, vspec],
        out_specs=qspec,
        scratch_shapes=[pltpu.VMEM(chains + (t, SB_HEAD_DIM), F32),
                        pltpu.VMEM(chains + (t, SB_SUB), F32)],
        compiler_params=_params(3),
        name="sb_attention",
    )(qkv, qkv, qkv)


def _ssd_kernel(xc_ref, sc_ref, dsk_ref, o_ref, st_ref):
    n = SSM_CHUNK
    n_seq = xc_ref.shape[0]

    @pl.when(pl.program_id(0) == 0)
    def _():
        st_ref[...] = jnp.zeros_like(st_ref)

    rows = lax.broadcasted_iota(jnp.int32, (n, n), 0)
    cols = lax.broadcasted_iota(jnp.int32, (n, n), 1)
    causal = rows >= cols
    low = cols < SSM_HEAD_DIM
    low_row = low[0:1, :]
    b_off = SSM_D_INNER
    c_off = SSM_D_INNER + SSM_GROUPS * SSM_STATE

    def scalars(b):
        names = ("cum", "cum_t", "ecum", "dt_t", "w_t")
        return {name: sc_ref[b, :, k * LANES:(k + 1) * LANES] for k, name in enumerate(names)}

    per_seq = [scalars(b) for b in range(n_seq)]
    group_cache = {}

    def group(b, g):
        if (b, g) not in group_cache:
            bg = xc_ref[b, :, b_off + g * SSM_STATE:b_off + (g + 1) * SSM_STATE]
            cg = xc_ref[b, :, c_off + g * SSM_STATE:c_off + (g + 1) * SSM_STATE]
            cb = lax.dot_general(cg, bg, (((1,), (1,)), ((), ())), preferred_element_type=F32)
            group_cache[b, g] = (cb, cg.astype(F32), bg.astype(F32).T)
        return group_cache[b, g]

    for p in range(SSM_HEADS // 2):
        lanes = slice(p * LANES, (p + 1) * LANES)
        for b in range(n_seq):
            s = per_seq[b]
            state = st_ref[b, p]
            xs = xc_ref[b, :, lanes].astype(F32)
            xs_pair = [jnp.where(low, xs, 0.0).astype(BF16), jnp.where(low, 0.0, xs).astype(BF16)]
            st_pair = [jnp.where(low, state, 0.0).astype(BF16),
                       jnp.where(low, 0.0, state).astype(BF16)]
            g_mats, c_mats, b_mats = [], [], []
            for h in (2 * p, 2 * p + 1):
                cb, cg, bg_t = group(b, h // SSM_HEADS_PER_GROUP)
                decay = jnp.where(
                    causal, jnp.exp(s["cum"][:, h:h + 1] - s["cum_t"][h:h + 1, :]), 0.0)
                g_mats.append((cb * decay * s["dt_t"][h:h + 1, :]).astype(BF16))
                c_mats.append((cg * s["ecum"][:, h:h + 1]).astype(BF16))
                b_mats.append((bg_t * s["w_t"][h:h + 1, :]).astype(BF16))
            y = _dot(jnp.concatenate(g_mats + c_mats, axis=1),
                     jnp.concatenate(xs_pair + st_pair, axis=0))
            o_ref[b, :, lanes] = (y + dsk_ref[:, lanes] * xs).astype(o_ref.dtype)
            last = s["ecum"][n - 1:n, :]
            state_decay = jnp.where(low_row, last[:, 2 * p:2 * p + 1],
                                    last[:, 2 * p + 1:2 * p + 2])
            st_ref[b, p] = state * state_decay + _dot(jnp.concatenate(b_mats, axis=1),
                                                      jnp.concatenate(xs_pair, axis=0))


def _ssd(xc, scalars, d_skip, batch, seq):
    n = SSM_CHUNK
    block = lambda w: pl.BlockSpec((batch, n, w), lambda c: (0, c, 0))
    return pl.pallas_call(
        _ssd_kernel,
        out_shape=jax.ShapeDtypeStruct((batch, seq, SSM_D_INNER), BF16),
        grid=(seq // n,),
        in_specs=[block(SSM_CONV_DIM), block(SCAN_SCALARS * LANES), _resident((1, SSM_D_INNER))],
        out_specs=block(SSM_D_INNER),
        scratch_shapes=[pltpu.VMEM((batch, SSM_HEADS // 2, SSM_STATE, LANES), F32)],
        compiler_params=_params(1),
        name="ssd",
    )(xc.reshape(batch, seq, -1), scalars.reshape(batch, seq, -1),
      jnp.repeat(d_skip, SSM_HEAD_DIM).reshape(1, -1)).reshape(batch * seq, SSM_D_INNER)


def _merge_kernel(x_ref, ysb_ref, y_ref, z_ref, gates_ref, nrm_ref, wsb_ref, wssm_ref,
                  wo_ref, o_ref, yn_ref):
    tm = MERGE_TOKEN_TILE
    low = lax.broadcasted_iota(jnp.int32, (tm, LANES), 1) < SSM_HEAD_DIM
    rsum = lambda v: jnp.sum(v, axis=-1, keepdims=True)
    y_sb = _dot(ysb_ref[...], wsb_ref[...])
    y_ssm = None
    pairs_per_part = SSM_GROUPS // 4
    for gp in range(SSM_GROUPS // 2):
        tiles = [slice((3 * gp + i) * LANES, (3 * gp + i + 1) * LANES) for i in range(3)]
        ys = [y_ref[:, s].astype(F32) * _silu(z_ref[:, s].astype(F32)) for s in tiles]
        sq = [v * v for v in ys]
        ss_a = rsum(sq[0]) + rsum(jnp.where(low, sq[1], 0.0))
        ss_b = rsum(sq[2]) + rsum(jnp.where(low, 0.0, sq[1]))
        r_a = lax.rsqrt(ss_a * (1.0 / SSM_GROUP_WIDTH) + NORM_EPS)
        r_b = lax.rsqrt(ss_b * (1.0 / SSM_GROUP_WIDTH) + NORM_EPS)
        scales = [r_a, jnp.where(low, r_a, r_b), r_b]
        for s, v, r in zip(tiles, ys, scales):
            yn_ref[:, s] = (v * r * nrm_ref[:, s]).astype(BF16)
        if (gp + 1) % pairs_per_part == 0:
            part = slice((gp + 1 - pairs_per_part) * 3 * LANES, (gp + 1) * 3 * LANES)
            term = _dot(yn_ref[:, part], wssm_ref[part, :])
            y_ssm = term if y_ssm is None else y_ssm + term
    merged = (jax.nn.sigmoid(gates_ref[:, :D_MODEL].astype(F32)) * y_sb
              + jax.nn.sigmoid(gates_ref[:, D_MODEL:].astype(F32)) * y_ssm)
    o_ref[...] = x_ref[...] + _dot(merged.astype(BF16), wo_ref[...])


def _merge(x, y_sb, y_scan, z, gates, ssm_norm, w_sb, w_ssm, w_out):
    t, d = x.shape
    tm = MERGE_TOKEN_TILE
    row = lambda w: pl.BlockSpec((tm, w), lambda i: (i, 0))
    return pl.pallas_call(
        _merge_kernel,
        out_shape=jax.ShapeDtypeStruct((t, d), F32),
        grid=(t // tm,),
        in_specs=[row(d), row(SB_WIDTH), row(SSM_D_INNER), row(SSM_D_INNER), row(2 * d),
                  _resident((1, SSM_D_INNER)), _resident((SB_WIDTH, d)),
                  _resident((SSM_D_INNER, d)), _resident((d, d))],
        out_specs=row(d),
        scratch_shapes=[pltpu.VMEM((tm, SSM_D_INNER), BF16)],
        compiler_params=_params(1),
        name="merge",
    )(x, y_sb, y_scan, z, gates, ssm_norm.reshape(1, -1), w_sb.astype(BF16),
      w_ssm.astype(BF16), w_out.astype(BF16))


def kernel(x, ffn1_norm, ffn1_w_up, ffn1_w_down, mix_norm, w_in, conv_w, conv_b, dt_bias,
           a_log, d_skip, ssm_norm, w_branch_sb, w_branch_ssm, w_out, ffn2_norm, ffn2_w_up,
           ffn2_w_down, final_norm):
    batch, seq, d = x.shape
    depth = w_in.shape[0]
    assert d == D_MODEL and seq % (SB_TILE * SB_QTILES_PER_STEP) == 0
    assert seq % SSM_CHUNK == 0
    assert seq % PROJ_TOKEN_TILE == 0 and (batch * seq) % FFN_TOKEN_TILE == 0
    xf = x.reshape(batch * seq, d)
    for l in range(depth):
        xf = _ffn(xf, ffn1_norm[l], ffn1_w_up[l], ffn1_w_down[l], None)
        qkv, z, xc, scalars, gates = _in_proj(
            xf, mix_norm[l], w_in[l], conv_w[l], conv_b[l], dt_bias[l], a_log[l], seq)
        y_sb = _sb_attention(qkv, batch, seq)
        y_scan = _ssd(xc, scalars, d_skip[l], batch, seq)
        xf = _merge(xf, y_sb, y_scan, z, gates, ssm_norm[l], w_branch_sb[l],
                    w_branch_ssm[l], w_out[l])
        xf = _ffn(xf, ffn2_norm[l], ffn2_w_up[l], ffn2_w_down[l],
                  final_norm if l == depth - 1 else None)
    return xf.reshape(batch, seq, d)
```

```python
import functools

import jax
import jax.numpy as jnp
from jax import lax
from jax.experimental import pallas as pl
from jax.experimental.pallas import tpu as pltpu

F32 = jnp.float32
BF16 = jnp.bfloat16

D_MODEL = 1024
SB_HEADS = 4
SB_HEAD_DIM = 128
SB_WIDTH = SB_HEADS * SB_HEAD_DIM
SSM_D_INNER = 1536
SSM_HEAD_DIM = 64
SSM_HEADS = SSM_D_INNER // SSM_HEAD_DIM
SSM_GROUPS = 8
SSM_HEADS_PER_GROUP = SSM_HEADS // SSM_GROUPS
SSM_STATE = 128
SSM_CONV = 4
SSM_CONV_DIM = SSM_D_INNER + 2 * SSM_GROUPS * SSM_STATE
SSM_GROUP_WIDTH = SSM_D_INNER // SSM_GROUPS
FFN_HIDDEN = 2816
NORM_EPS = 1e-6

LANES = 128
SUBLANES = 8
MXU_DIM = 256
VMEM_LIMIT = 56 * 1024 * 1024

FFN_TOKEN_TILE = 512
FFN_HIDDEN_CHUNKS = (5 * MXU_DIM, 6 * MXU_DIM)
PROJ_TOKEN_TILE = 256
MERGE_TOKEN_TILE = 512
SB_TILE = 256
SB_SUB = 128
SB_HEADS_PER_STEP = 2
SB_QTILES_PER_STEP = 2
LOG_WEIGHT_FLOOR = -105.0
DEAD_TOTAL = -1e30
SSM_CHUNK = 128
SCAN_SCALARS = 5
CONV_COL_TILE = 256
HALO = SUBLANES

assert sum(FFN_HIDDEN_CHUNKS) == FFN_HIDDEN


def _params(n_axes, flags=None):
    return pltpu.CompilerParams(
        dimension_semantics=("arbitrary",) * n_axes, vmem_limit_bytes=VMEM_LIMIT, flags=flags)


def _resident(shape):
    zeros = (0,) * len(shape)
    return pl.BlockSpec(shape, lambda *_: zeros, pipeline_mode=pl.Buffered(1))


def _layer_resident(stacked, layer):
    zeros = (0,) * (stacked.ndim - 1)
    return pl.BlockSpec((None,) + stacked.shape[1:], lambda *_: (layer,) + zeros,
                        pipeline_mode=pl.Buffered(1))


def _rms_norm(x, gain):
    ms = jnp.mean(x * x, axis=-1, keepdims=True)
    return x * lax.rsqrt(ms + NORM_EPS) * gain


def _silu(x):
    return x * jax.nn.sigmoid(x)


def _softplus(x):
    return jnp.maximum(x, 0.0) + jnp.log1p(jnp.exp(-jnp.abs(x)))


def _dot(a, b):
    return jnp.dot(a, b, preferred_element_type=F32)


def _split_bf16(x, parts):
    out = []
    for _ in range(parts - 1):
        hi = x.astype(BF16)
        out.append(hi)
        x = x - hi.astype(F32)
    out.append(x.astype(BF16))
    return out


def _ffn_kernel(x_ref, g_ref, wu_ref, wd_ref, *rest, final_norm):
    if final_norm:
        fg_ref, o_ref = rest
    else:
        (o_ref,) = rest
    x = x_ref[...]
    xn = _rms_norm(x, g_ref[...]).astype(BF16)
    acc, start = None, 0
    for width in FFN_HIDDEN_CHUNKS:
        cols = slice(start, start + width)
        start += width
        gate = _dot(xn, wu_ref[:, cols])
        up = _dot(xn, wu_ref[:, FFN_HIDDEN + cols.start:FFN_HIDDEN + cols.stop])
        act = (_silu(gate) * up).astype(BF16)
        part = _dot(act, wd_ref[cols, :])
        acc = part if acc is None else acc + part
    y = x + 0.5 * acc
    if final_norm:
        y = _rms_norm(y, fg_ref[...])
    o_ref[...] = y


def _ffn(x, gain, w_up, w_down, layer, final_gain):
    t, d = x.shape
    tm = FFN_TOKEN_TILE
    row = pl.BlockSpec((tm, d), lambda i: (i, 0))
    in_specs = [row, _resident((1, d)), _layer_resident(w_up, layer),
                _layer_resident(w_down, layer)]
    args = [x, gain.reshape(1, d), w_up, w_down]
    if final_gain is not None:
        in_specs.append(_resident((1, d)))
        args.append(final_gain.reshape(1, d))
    return pl.pallas_call(
        functools.partial(_ffn_kernel, final_norm=final_gain is not None),
        out_shape=jax.ShapeDtypeStruct((t, d), F32),
        grid=(t // tm,),
        in_specs=in_specs,
        out_specs=row,
        compiler_params=_params(1),
        name="ffn",
    )(*args)


def _in_proj_kernel(x_ref, g_ref, w_ref, wdt_ref, wgate_ref, convw_ref, convb_ref, dtb_ref,
                    alog_ref, qkv_ref, z_ref, xc_ref, sc_ref, gates_ref, xpad_ref, *,
                    tiles_per_seq):
    tm = PROJ_TOKEN_TILE
    n = SSM_CHUNK
    z_off = 3 * SB_WIDTH
    x_off = z_off + SSM_D_INNER

    @pl.when(pl.program_id(0) == 0)
    def _():
        xpad_ref[...] = jnp.zeros_like(xpad_ref)

    xn = _rms_norm(x_ref[...], g_ref[...]).astype(BF16)
    starts_seq = lax.rem(pl.program_id(0), tiles_per_seq) == 0

    def project(src_ref, src_off, width, o_ref, offset=0, scale=None):
        def tile(c):
            y = _dot(xn, src_ref[:, src_off + c:src_off + c + CONV_COL_TILE])
            o_ref[:, offset + c:offset + c + CONV_COL_TILE] = (
                y if scale is None else y * scale).astype(o_ref.dtype)
        return [functools.partial(tile, c) for c in range(0, width, CONV_COL_TILE)]

    def scan_scalars():
        dt_all = _softplus(_dot(xn, wdt_ref[...]) + dtb_ref[...])
        neg_a = -jnp.exp(alog_ref[...])
        incl = (lax.broadcasted_iota(jnp.int32, (n, n), 0)
                >= lax.broadcasted_iota(jnp.int32, (n, n), 1)).astype(BF16)
        incl3 = jnp.concatenate([incl] * 3, axis=1)
        for c in range(tm // n):
            rs = slice(c * n, (c + 1) * n)
            dt = dt_all[rs, :]
            cum = _dot(incl3, jnp.concatenate(_split_bf16(dt * neg_a, 3), axis=0))
            parts = [cum, cum.T, jnp.exp(cum), dt.T, (dt * jnp.exp(cum[n - 1:n, :] - cum)).T]
            for k, part in enumerate(parts):
                sc_ref[rs, k * LANES:(k + 1) * LANES] = part

    plain = ([scan_scalars] + project(w_ref, 0, SB_WIDTH, qkv_ref, 0, SB_HEAD_DIM ** -0.5)
             + project(w_ref, SB_WIDTH, 2 * SB_WIDTH, qkv_ref, SB_WIDTH)
             + project(w_ref, z_off, SSM_D_INNER, z_ref)
             + project(wgate_ref, 0, 2 * D_MODEL, gates_ref))

    def conv_tile(cs, y, tail):
        ext = jnp.concatenate([tail, y], axis=0)
        acc = convb_ref[:, cs]
        for i in range(SSM_CONV - 1):
            shifted = pltpu.roll(ext, SSM_CONV - 1 - i, axis=0)[HALO:, :]
            acc = acc + convw_ref[i:i + 1, cs] * shifted
        acc = acc + convw_ref[SSM_CONV - 1:SSM_CONV, cs] * y
        xc_ref[:, cs] = _silu(acc).astype(BF16)

    n_tiles = SSM_CONV_DIM // CONV_COL_TILE
    pending = None
    for ct in range(n_tiles):
        cs = slice(ct * CONV_COL_TILE, (ct + 1) * CONV_COL_TILE)
        y = _dot(xn, w_ref[:, x_off + cs.start:x_off + cs.stop])
        tail = jnp.where(starts_seq, 0.0, xpad_ref[:, cs])
        xpad_ref[:, cs] = y[tm - HALO:tm, :]
        for task in plain[ct::n_tiles]:
            task()
        if pending is not None:
            conv_tile(*pending)
        pending = (cs, y, tail)
    conv_tile(*pending)


def _in_proj(x, gain, w_in, w_dt, w_gates, layer, conv_w, conv_b, dt_bias, a_log, seq):
    t, d = x.shape
    tm = PROJ_TOKEN_TILE
    pad = lambda v: jnp.pad(v, (0, LANES - SSM_HEADS)).reshape(1, LANES)
    out_widths = (3 * SB_WIDTH, SSM_D_INNER, SSM_CONV_DIM, SCAN_SCALARS * LANES, 2 * D_MODEL)
    out_dtypes = (BF16, BF16, BF16, F32, BF16)
    row = lambda w: pl.BlockSpec((tm, w), lambda i: (i, 0))
    return pl.pallas_call(
        functools.partial(_in_proj_kernel, tiles_per_seq=seq // tm),
        out_shape=[jax.ShapeDtypeStruct((t, w), dt) for w, dt in zip(out_widths, out_dtypes)],
        grid=(t // tm,),
        in_specs=[row(d), _resident((1, d)), _layer_resident(w_in, layer),
                  _layer_resident(w_dt, layer), _layer_resident(w_gates, layer),
                  _resident((SSM_CONV, SSM_CONV_DIM)), _resident((1, SSM_CONV_DIM)),
                  _resident((1, LANES)), _resident((1, LANES))],
        out_specs=[row(w) for w in out_widths],
        scratch_shapes=[pltpu.VMEM((HALO, SSM_CONV_DIM), F32)],
        compiler_params=_params(1),
        name="in_proj",
    )(x, gain.reshape(1, d), w_in, w_dt, w_gates, conv_w, conv_b.reshape(1, -1), pad(dt_bias),
      pad(a_log))


def _sb_prepare(z, blocks):
    s = SB_SUB
    out = []
    for b, mask in blocks:
        zb = z[:, b * s:(b + 1) * s]
        tail = jnp.log(1.0 + jnp.exp(-jnp.abs(zb)))
        log_keep = -(jnp.maximum(zb, 0.0) + tail)
        if mask is not None:
            log_keep = jnp.where(mask, log_keep, 0.0)
        out.append((b, mask, jnp.minimum(zb, 0.0) - tail,
                    jnp.concatenate(_split_bf16(log_keep, 2), axis=1)))
    return out


def _sb_finish(prepared, sums, vb, total):
    s = SB_SUB
    weights = {}
    for (b, mask, log_beta, _), block_sums in zip(prepared, sums):
        w = jnp.exp(log_beta + block_sums[:, :s] + total)
        if mask is not None:
            w = jnp.where(mask, w, 0.0)
        weights[b] = w.astype(BF16)
        total = total + block_sums[:, s:]
    order = sorted(weights)
    w_all = jnp.concatenate([weights[b] for b in order], axis=1)
    return _dot(w_all, vb[order[0] * s:(order[-1] + 1) * s, :]), total


def _sb_kernel(q_ref, k_ref, v_ref, o_ref, acc_ref, tot_ref):
    i = pl.program_id(2)
    t, s = SB_TILE, SB_SUB
    rows = lax.broadcasted_iota(jnp.int32, (s, s), 0)
    cols = lax.broadcasted_iota(jnp.int32, (s, s), 1)
    later = (rows > cols).astype(BF16)
    half = jnp.concatenate([later, jnp.ones((s, s), BF16)], axis=1)
    sum_mat = jnp.concatenate([half, half], axis=0)
    before = cols < rows

    def visit(back, diagonal):
        work = []
        for a in range(SB_QTILES_PER_STEP):
            rs = slice(a * t, (a + 1) * t)
            j = i * SB_QTILES_PER_STEP + a - back
            start = pl.multiple_of(jnp.maximum(j, 0) * t, t)
            for h in range(SB_HEADS_PER_STEP):
                hs = slice(h * SB_HEAD_DIM, (h + 1) * SB_HEAD_DIM)
                z = lax.dot_general(q_ref[rs, hs], k_ref[pl.ds(start, t), hs],
                                    (((1,), (1,)), ((), ())), preferred_element_type=F32)
                if diagonal:
                    zero = jnp.zeros((s, s), F32)
                    units = [(_sb_prepare(z[:s], [(0, before)]), zero),
                             (_sb_prepare(z[s:], [(1, before), (0, None)]), zero)]
                else:
                    total = jnp.where(j >= 0, tot_ref[a, h], DEAD_TOTAL)
                    units = [(_sb_prepare(z, [(1, None), (0, None)]), total)]
                work.append((a, h, j, v_ref[pl.ds(start, t), hs], units))

        operands = [blk[3] for _, _, _, _, units in work for prep, _ in units for blk in prep]
        all_sums = _dot(jnp.concatenate(operands, axis=0), sum_mat)

        largest, row = None, 0
        for a, h, j, vb, units in work:
            parts, totals = [], []
            for prepared, total in units:
                sums = []
                for blk in prepared:
                    sums.append(all_sums[row:row + blk[3].shape[0], :])
                    row += blk[3].shape[0]
                part, total = _sb_finish(prepared, sums, vb, total)
                parts.append(part)
                totals.append(total)
            part = jnp.concatenate(parts, axis=0)
            total = jnp.concatenate(totals, axis=0)
            acc_ref[a, h] = part if diagonal else acc_ref[a, h] + part
            tot_ref[a, h] = total
            top = jnp.where(j >= 1, jnp.max(total), DEAD_TOTAL)
            largest = top if largest is None else jnp.maximum(largest, top)
        return largest

    def cond(carry):
        return carry[1] > LOG_WEIGHT_FLOOR

    def body(carry):
        return carry[0] + 1, visit(carry[0], False)

    lax.while_loop(cond, body, (jnp.int32(1), visit(0, True)))
    for a in range(SB_QTILES_PER_STEP):
        for h in range(SB_HEADS_PER_STEP):
            o_ref[a * t:(a + 1) * t, h * SB_HEAD_DIM:(h + 1) * SB_HEAD_DIM] = (
                acc_ref[a, h].astype(o_ref.dtype))


def _sb_attention(qkv, batch, seq):
    t = SB_TILE
    rows = SB_QTILES_PER_STEP * t
    nq = seq // rows
    width = SB_HEADS_PER_STEP * SB_HEAD_DIM
    groups = SB_HEADS // SB_HEADS_PER_STEP
    qspec = pl.BlockSpec((rows, width), lambda b, g, i: (b * nq + i, g))
    kspec = pl.BlockSpec((seq, width), lambda b, g, i: (b, groups + g))
    vspec = pl.BlockSpec((seq, width), lambda b, g, i: (b, 2 * groups + g))
    chains = (SB_QTILES_PER_STEP, SB_HEADS_PER_STEP)
    return pl.pallas_call(
        _sb_kernel,
        out_shape=jax.ShapeDtypeStruct((batch * seq, SB_WIDTH), BF16),
        grid=(batch, groups, nq),
        in_specs=[qspec, kspec, vspec],
        out_specs=qspec,
        scratch_shapes=[pltpu.VMEM(chains + (t, SB_HEAD_DIM), F32),
                        pltpu.VMEM(chains + (t, SB_SUB), F32)],
        compiler_params=_params(3),
        name="sb_attention",
    )(qkv, qkv, qkv)


def _ssd_kernel(xc_ref, sc_ref, dsk_ref, o_ref, st_ref):
    n = SSM_CHUNK
    n_seq = xc_ref.shape[0]

    @pl.when(pl.program_id(0) == 0)
    def _():
        st_ref[...] = jnp.zeros_like(st_ref)

    rows = lax.broadcasted_iota(jnp.int32, (n, n), 0)
    cols = lax.broadcasted_iota(jnp.int32, (n, n), 1)
    causal = rows >= cols
    low = cols < SSM_HEAD_DIM
    low_row = low[0:1, :]
    b_off = SSM_D_INNER
    c_off = SSM_D_INNER + SSM_GROUPS * SSM_STATE

    def scalars(b):
        names = ("cum", "cum_t", "ecum", "dt_t", "w_t")
        return {name: sc_ref[b, :, k * LANES:(k + 1) * LANES] for k, name in enumerate(names)}

    per_seq = [scalars(b) for b in range(n_seq)]
    group_cache = {}

    def group(b, g):
        if (b, g) not in group_cache:
            bg = xc_ref[b, :, b_off + g * SSM_STATE:b_off + (g + 1) * SSM_STATE]
            cg = xc_ref[b, :, c_off + g * SSM_STATE:c_off + (g + 1) * SSM_STATE]
            cb = lax.dot_general(cg, bg, (((1,), (1,)), ((), ())), preferred_element_type=F32)
            group_cache[b, g] = (cb, cg.astype(F32), bg.astype(F32).T)
        return group_cache[b, g]

    for p in range(SSM_HEADS // 2):
        lanes = slice(p * LANES, (p + 1) * LANES)
        for b in range(n_seq):
            s = per_seq[b]
            state = st_ref[b, p]
            xs = xc_ref[b, :, lanes].astype(F32)
            xs_pair = [jnp.where(low, xs, 0.0).astype(BF16), jnp.where(low, 0.0, xs).astype(BF16)]
            st_pair = [jnp.where(low, state, 0.0).astype(BF16),
                       jnp.where(low, 0.0, state).astype(BF16)]
            g_mats, c_mats, b_mats = [], [], []
            for h in (2 * p, 2 * p + 1):
                cb, cg, bg_t = group(b, h // SSM_HEADS_PER_GROUP)
                decay = jnp.where(
                    causal, jnp.exp(s["cum"][:, h:h + 1] - s["cum_t"][h:h + 1, :]), 0.0)
                g_mats.append((cb * decay * s["dt_t"][h:h + 1, :]).astype(BF16))
                c_mats.append((cg * s["ecum"][:, h:h + 1]).astype(BF16))
                b_mats.append((bg_t * s["w_t"][h:h + 1, :]).astype(BF16))
            y = _dot(jnp.concatenate(g_mats + c_mats, axis=1),
                     jnp.concatenate(xs_pair + st_pair, axis=0))
            o_ref[b, :, lanes] = (y + dsk_ref[:, lanes] * xs).astype(o_ref.dtype)
            last = s["ecum"][n - 1:n, :]
            state_decay = jnp.where(low_row, last[:, 2 * p:2 * p + 1],
                                    last[:, 2 * p + 1:2 * p + 2])
            st_ref[b, p] = state * state_decay + _dot(jnp.concatenate(b_mats, axis=1),
                                                      jnp.concatenate(xs_pair, axis=0))


def _ssd(xc, scalars, d_skip, batch, seq):
    n = SSM_CHUNK
    block = lambda w: pl.BlockSpec((batch, n, w), lambda c: (0, c, 0))
    return pl.pallas_call(
        _ssd_kernel,
        out_shape=jax.ShapeDtypeStruct((batch, seq, SSM_D_INNER), BF16),
        grid=(seq // n,),
        in_specs=[block(SSM_CONV_DIM), block(SCAN_SCALARS * LANES), _resident((1, SSM_D_INNER))],
        out_specs=block(SSM_D_INNER),
        scratch_shapes=[pltpu.VMEM((batch, SSM_HEADS // 2, SSM_STATE, LANES), F32)],
        compiler_params=_params(1),
        name="ssd",
    )(xc.reshape(batch, seq, -1), scalars.reshape(batch, seq, -1),
      jnp.repeat(d_skip, SSM_HEAD_DIM).reshape(1, -1)).reshape(batch * seq, SSM_D_INNER)


def _merge_kernel(x_ref, ysb_ref, y_ref, z_ref, gates_ref, nrm_ref, wsb_ref, wssm_ref,
                  wo_ref, o_ref, yn_ref):
    tm = MERGE_TOKEN_TILE
    low = lax.broadcasted_iota(jnp.int32, (tm, LANES), 1) < SSM_HEAD_DIM
    rsum = lambda v: jnp.sum(v, axis=-1, keepdims=True)
    y_sb = _dot(ysb_ref[...], wsb_ref[...])
    y_ssm = None
    pairs_per_part = SSM_GROUPS // 4
    for gp in range(SSM_GROUPS // 2):
        tiles = [slice((3 * gp + i) * LANES, (3 * gp + i + 1) * LANES) for i in range(3)]
        ys = [y_ref[:, s].astype(F32) * _silu(z_ref[:, s].astype(F32)) for s in tiles]
        sq = [v * v for v in ys]
        ss_a = rsum(sq[0]) + rsum(jnp.where(low, sq[1], 0.0))
        ss_b = rsum(sq[2]) + rsum(jnp.where(low, 0.0, sq[1]))
        r_a = lax.rsqrt(ss_a * (1.0 / SSM_GROUP_WIDTH) + NORM_EPS)
        r_b = lax.rsqrt(ss_b * (1.0 / SSM_GROUP_WIDTH) + NORM_EPS)
        scales = [r_a, jnp.where(low, r_a, r_b), r_b]
        for s, v, r in zip(tiles, ys, scales):
            yn_ref[:, s] = (v * r * nrm_ref[:, s]).astype(BF16)
        if (gp + 1) % pairs_per_part == 0:
            part = slice((gp + 1 - pairs_per_part) * 3 * LANES, (gp + 1) * 3 * LANES)
            term = _dot(yn_ref[:, part], wssm_ref[part, :])
            y_ssm = term if y_ssm is None else y_ssm + term
    merged = (jax.nn.sigmoid(gates_ref[:, :D_MODEL].astype(F32)) * y_sb
              + jax.nn.sigmoid(gates_ref[:, D_MODEL:].astype(F32)) * y_ssm)
    o_ref[...] = x_ref[...] + _dot(merged.astype(BF16), wo_ref[...])


def _merge(x, y_sb, y_scan, z, gates, ssm_norm, w_sb, w_ssm, w_out, layer):
    t, d = x.shape
    tm = MERGE_TOKEN_TILE
    row = lambda w: pl.BlockSpec((tm, w), lambda i: (i, 0))
    return pl.pallas_call(
        _merge_kernel,
        out_shape=jax.ShapeDtypeStruct((t, d), F32),
        grid=(t // tm,),
        in_specs=[row(d), row(SB_WIDTH), row(SSM_D_INNER), row(SSM_D_INNER), row(2 * d),
                  _resident((1, SSM_D_INNER)), _layer_resident(w_sb, layer),
                  _layer_resident(w_ssm, layer), _layer_resident(w_out, layer)],
        out_specs=row(d),
        scratch_shapes=[pltpu.VMEM((tm, SSM_D_INNER), BF16)],
        compiler_params=_params(1),
        name="merge",
    )(x, y_sb, y_scan, z, gates, ssm_norm.reshape(1, -1), w_sb, w_ssm, w_out)


def kernel(x, ffn1_norm, ffn1_w_up, ffn1_w_down, mix_norm, w_in, conv_w, conv_b, dt_bias,
           a_log, d_skip, ssm_norm, w_branch_sb, w_branch_ssm, w_out, ffn2_norm, ffn2_w_up,
           ffn2_w_down, final_norm):
    batch, seq, d = x.shape
    depth = w_in.shape[0]
    assert d == D_MODEL and seq % (SB_TILE * SB_QTILES_PER_STEP) == 0
    assert seq % SSM_CHUNK == 0
    assert seq % PROJ_TOKEN_TILE == 0 and (batch * seq) % FFN_TOKEN_TILE == 0
    bf16 = lambda w: w.astype(BF16)
    up1, down1, up2, down2 = map(bf16, (ffn1_w_up, ffn1_w_down, ffn2_w_up, ffn2_w_down))
    w_sb, w_ssm, w_o, w_all = map(bf16, (w_branch_sb, w_branch_ssm, w_out, w_in))
    dt_off = 3 * SB_WIDTH + SSM_D_INNER + SSM_CONV_DIM
    w_dt = jnp.pad(w_all[:, :, dt_off:dt_off + SSM_HEADS],
                   ((0, 0), (0, 0), (0, LANES - SSM_HEADS)))
    w_gates = w_all[:, :, dt_off + SSM_HEADS:]
    xf = x.reshape(batch * seq, d)
    for l in range(depth):
        xf = _ffn(xf, ffn1_norm[l], up1, down1, l, None)
        qkv, z, xc, scalars, gates = _in_proj(
            xf, mix_norm[l], w_all, w_dt, w_gates, l, conv_w[l], conv_b[l], dt_bias[l],
            a_log[l], seq)
        y_sb = _sb_attention(qkv, batch, seq)
        y_scan = _ssd(xc, scalars, d_skip[l], batch, seq)
        xf = _merge(xf, y_sb, y_scan, z, gates, ssm_norm[l], w_sb, w_ssm, w_o, l)
        xf = _ffn(xf, ffn2_norm[l], up2, down2, l, final_norm if l == depth - 1 else None)
    return xf.reshape(batch, seq, d)
```

```python
import functools

import jax
import jax.numpy as jnp
from jax import lax
from jax.experimental import pallas as pl
from jax.experimental.pallas import tpu as pltpu

F32 = jnp.float32
BF16 = jnp.bfloat16

D_MODEL = 1024
SB_HEADS = 4
SB_HEAD_DIM = 128
SB_WIDTH = SB_HEADS * SB_HEAD_DIM
SSM_D_INNER = 1536
SSM_HEAD_DIM = 64
SSM_HEADS = SSM_D_INNER // SSM_HEAD_DIM
SSM_GROUPS = 8
SSM_HEADS_PER_GROUP = SSM_HEADS // SSM_GROUPS
SSM_STATE = 128
SSM_CONV = 4
SSM_CONV_DIM = SSM_D_INNER + 2 * SSM_GROUPS * SSM_STATE
SSM_GROUP_WIDTH = SSM_D_INNER // SSM_GROUPS
FFN_HIDDEN = 2816
NORM_EPS = 1e-6

LANES = 128
SUBLANES = 8
MXU_DIM = 256
VMEM_LIMIT = 56 * 1024 * 1024

FFN_TOKEN_TILE = 512
FFN_HIDDEN_CHUNKS = (5 * MXU_DIM, 6 * MXU_DIM)
PROJ_TOKEN_TILE = 256
MERGE_TOKEN_TILE = 512
SB_TILE = 256
SB_SUB = 128
SB_HEADS_PER_STEP = 2
SB_QTILES_PER_STEP = 2
LOG_WEIGHT_FLOOR = -105.0
DEAD_TOTAL = -1e30
SSM_CHUNK = 128
SCAN_SCALARS = 5
CONV_COL_TILE = 256
HALO = SUBLANES

assert sum(FFN_HIDDEN_CHUNKS) == FFN_HIDDEN


def _params(n_axes, flags=None):
    return pltpu.CompilerParams(
        dimension_semantics=("arbitrary",) * n_axes, vmem_limit_bytes=VMEM_LIMIT, flags=flags)


def _resident(shape):
    zeros = (0,) * len(shape)
    return pl.BlockSpec(shape, lambda *_: zeros, pipeline_mode=pl.Buffered(1))


def _layer_resident(stacked, layer):
    zeros = (0,) * (stacked.ndim - 1)
    return pl.BlockSpec((None,) + stacked.shape[1:], lambda *_: (layer,) + zeros,
                        pipeline_mode=pl.Buffered(1))


def _rms_norm(x, gain):
    ms = jnp.mean(x * x, axis=-1, keepdims=True)
    return x * lax.rsqrt(ms + NORM_EPS) * gain


def _silu(x):
    return x * jax.nn.sigmoid(x)


def _softplus(x):
    return jnp.maximum(x, 0.0) + jnp.log1p(jnp.exp(-jnp.abs(x)))


def _dot(a, b):
    return jnp.dot(a, b, preferred_element_type=F32)


def _split_bf16(x, parts):
    out = []
    for _ in range(parts - 1):
        hi = x.astype(BF16)
        out.append(hi)
        x = x - hi.astype(F32)
    out.append(x.astype(BF16))
    return out


def _ffn_kernel(x_ref, g_ref, wu_ref, wd_ref, *rest, final_norm):
    if final_norm:
        fg_ref, o_ref = rest
    else:
        (o_ref,) = rest
    x = x_ref[...]
    xn = _rms_norm(x, g_ref[...]).astype(BF16)
    acc, start = None, 0
    for width in FFN_HIDDEN_CHUNKS:
        cols = slice(start, start + width)
        start += width
        gate = _dot(xn, wu_ref[:, cols])
        up = _dot(xn, wu_ref[:, FFN_HIDDEN + cols.start:FFN_HIDDEN + cols.stop])
        act = (_silu(gate) * up).astype(BF16)
        part = _dot(act, wd_ref[cols, :])
        acc = part if acc is None else acc + part
    y = x + 0.5 * acc
    if final_norm:
        y = _rms_norm(y, fg_ref[...])
    o_ref[...] = y


def _ffn(x, gain, w_up, w_down, layer, final_gain):
    t, d = x.shape
    tm = FFN_TOKEN_TILE
    row = pl.BlockSpec((tm, d), lambda i: (i, 0))
    in_specs = [row, _resident((1, d)), _layer_resident(w_up, layer),
                _layer_resident(w_down, layer)]
    args = [x, gain.reshape(1, d), w_up, w_down]
    if final_gain is not None:
        in_specs.append(_resident((1, d)))
        args.append(final_gain.reshape(1, d))
    return pl.pallas_call(
        functools.partial(_ffn_kernel, final_norm=final_gain is not None),
        out_shape=jax.ShapeDtypeStruct((t, d), F32),
        grid=(t // tm,),
        in_specs=in_specs,
        out_specs=row,
        compiler_params=_params(1),
        name="ffn",
    )(*args)


def _in_proj_kernel(x_ref, g_ref, w_ref, wdt_ref, wgate_ref, convw_ref, convb_ref, dtb_ref,
                    alog_ref, qkv_ref, z_ref, xc_ref, sc_ref, gates_ref, xpad_ref, *,
                    tiles_per_seq):
    tm = PROJ_TOKEN_TILE
    n = SSM_CHUNK
    z_off = 3 * SB_WIDTH
    x_off = z_off + SSM_D_INNER

    @pl.when(pl.program_id(0) == 0)
    def _():
        xpad_ref[...] = jnp.zeros_like(xpad_ref)

    xn = _rms_norm(x_ref[...], g_ref[...]).astype(BF16)
    starts_seq = lax.rem(pl.program_id(0), tiles_per_seq) == 0

    def project(src_ref, src_off, width, o_ref, offset=0, scale=None):
        def tile(c):
            y = _dot(xn, src_ref[:, src_off + c:src_off + c + CONV_COL_TILE])
            o_ref[:, offset + c:offset + c + CONV_COL_TILE] = (
                y if scale is None else y * scale).astype(o_ref.dtype)
        return [functools.partial(tile, c) for c in range(0, width, CONV_COL_TILE)]

    def scan_scalars():
        dt_all = _softplus(_dot(xn, wdt_ref[...]) + dtb_ref[...])
        neg_a = -jnp.exp(alog_ref[...])
        incl = (lax.broadcasted_iota(jnp.int32, (n, n), 0)
                >= lax.broadcasted_iota(jnp.int32, (n, n), 1)).astype(BF16)
        incl3 = jnp.concatenate([incl] * 3, axis=1)
        for c in range(tm // n):
            rs = slice(c * n, (c + 1) * n)
            dt = dt_all[rs, :]
            cum = _dot(incl3, jnp.concatenate(_split_bf16(dt * neg_a, 3), axis=0))
            parts = [cum, cum.T, jnp.exp(cum), dt.T, (dt * jnp.exp(cum[n - 1:n, :] - cum)).T]
            for k, part in enumerate(parts):
                sc_ref[rs, k * LANES:(k + 1) * LANES] = part

    plain = ([scan_scalars] + project(w_ref, 0, SB_WIDTH, qkv_ref, 0, SB_HEAD_DIM ** -0.5)
             + project(w_ref, SB_WIDTH, 2 * SB_WIDTH, qkv_ref, SB_WIDTH)
             + project(w_ref, z_off, SSM_D_INNER, z_ref)
             + project(wgate_ref, 0, 2 * D_MODEL, gates_ref))

    def conv_tile(cs, y, tail):
        ext = jnp.concatenate([tail, y], axis=0)
        acc = convb_ref[:, cs]
        for i in range(SSM_CONV - 1):
            shifted = pltpu.roll(ext, SSM_CONV - 1 - i, axis=0)[HALO:, :]
            acc = acc + convw_ref[i:i + 1, cs] * shifted
        acc = acc + convw_ref[SSM_CONV - 1:SSM_CONV, cs] * y
        xc_ref[:, cs] = _silu(acc).astype(BF16)

    n_tiles = SSM_CONV_DIM // CONV_COL_TILE
    pending = None
    for ct in range(n_tiles):
        cs = slice(ct * CONV_COL_TILE, (ct + 1) * CONV_COL_TILE)
        y = _dot(xn, w_ref[:, x_off + cs.start:x_off + cs.stop])
        tail = jnp.where(starts_seq, 0.0, xpad_ref[:, cs])
        xpad_ref[:, cs] = y[tm - HALO:tm, :]
        for task in plain[ct::n_tiles]:
            task()
        if pending is not None:
            conv_tile(*pending)
        pending = (cs, y, tail)
    conv_tile(*pending)


def _in_proj(x, gain, w_in, w_dt, w_gates, layer, conv_w, conv_b, dt_bias, a_log, seq):
    t, d = x.shape
    tm = PROJ_TOKEN_TILE
    pad = lambda v: jnp.pad(v, (0, LANES - SSM_HEADS)).reshape(1, LANES)
    out_widths = (3 * SB_WIDTH, SSM_D_INNER, SSM_CONV_DIM, SCAN_SCALARS * LANES, 2 * D_MODEL)
    out_dtypes = (BF16, BF16, BF16, F32, BF16)
    row = lambda w: pl.BlockSpec((tm, w), lambda i: (i, 0))
    return pl.pallas_call(
        functools.partial(_in_proj_kernel, tiles_per_seq=seq // tm),
        out_shape=[jax.ShapeDtypeStruct((t, w), dt) for w, dt in zip(out_widths, out_dtypes)],
        grid=(t // tm,),
        in_specs=[row(d), _resident((1, d)), _layer_resident(w_in, layer),
                  _layer_resident(w_dt, layer), _layer_resident(w_gates, layer),
                  _resident((SSM_CONV, SSM_CONV_DIM)), _resident((1, SSM_CONV_DIM)),
                  _resident((1, LANES)), _resident((1, LANES))],
        out_specs=[row(w) for w in out_widths],
        scratch_shapes=[pltpu.VMEM((HALO, SSM_CONV_DIM), F32)],
        compiler_params=_params(1),
        name="in_proj",
    )(x, gain.reshape(1, d), w_in, w_dt, w_gates, conv_w, conv_b.reshape(1, -1), pad(dt_bias),
      pad(a_log))


def _sb_prepare(z, blocks):
    s = SB_SUB
    out = []
    for b, mask in blocks:
        zb = z[:, b * s:(b + 1) * s]
        tail = jnp.log(1.0 + jnp.exp(-jnp.abs(zb)))
        log_keep = -(jnp.maximum(zb, 0.0) + tail)
        if mask is not None:
            log_keep = jnp.where(mask, log_keep, 0.0)
        out.append((b, mask, jnp.minimum(zb, 0.0) - tail,
                    jnp.concatenate(_split_bf16(log_keep, 2), axis=1)))
    return out


def _sb_finish(prepared, sums, vb, total):
    s = SB_SUB
    weights = {}
    for (b, mask, log_beta, _), block_sums in zip(prepared, sums):
        w = jnp.exp(log_beta + block_sums[:, :s] + total)
        if mask is not None:
            w = jnp.where(mask, w, 0.0)
        weights[b] = w.astype(BF16)
        total = total + block_sums[:, s:]
    order = sorted(weights)
    w_all = jnp.concatenate([weights[b] for b in order], axis=1)
    return _dot(w_all, vb[order[0] * s:(order[-1] + 1) * s, :]), total


def _sb_kernel(q_ref, k_ref, v_ref, o_ref, acc_ref, tot_ref):
    i = pl.program_id(2)
    t, s = SB_TILE, SB_SUB
    rows = lax.broadcasted_iota(jnp.int32, (s, s), 0)
    cols = lax.broadcasted_iota(jnp.int32, (s, s), 1)
    later = (rows > cols).astype(BF16)
    half = jnp.concatenate([later, jnp.ones((s, s), BF16)], axis=1)
    sum_mat = jnp.concatenate([half, half], axis=0)
    before = cols < rows

    def visit(back, diagonal):
        work = []
        for a in range(SB_QTILES_PER_STEP):
            rs = slice(a * t, (a + 1) * t)
            j = i * SB_QTILES_PER_STEP + a - back
            start = pl.multiple_of(jnp.maximum(j, 0) * t, t)
            for h in range(SB_HEADS_PER_STEP):
                hs = slice(h * SB_HEAD_DIM, (h + 1) * SB_HEAD_DIM)
                z = lax.dot_general(q_ref[rs, hs], k_ref[pl.ds(start, t), hs],
                                    (((1,), (1,)), ((), ())), preferred_element_type=F32)
                if diagonal:
                    zero = jnp.zeros((s, s), F32)
                    units = [(_sb_prepare(z[:s], [(0, before)]), zero),
                             (_sb_prepare(z[s:], [(1, before), (0, None)]), zero)]
                else:
                    total = jnp.where(j >= 0, tot_ref[a, h], DEAD_TOTAL)
                    units = [(_sb_prepare(z, [(1, None), (0, None)]), total)]
                work.append((a, h, j, v_ref[pl.ds(start, t), hs], units))

        operands = [blk[3] for _, _, _, _, units in work for prep, _ in units for blk in prep]
        all_sums = _dot(jnp.concatenate(operands, axis=0), sum_mat)

        largest, row = None, 0
        for a, h, j, vb, units in work:
            parts, totals = [], []
            for prepared, total in units:
                sums = []
                for blk in prepared:
                    sums.append(all_sums[row:row + blk[3].shape[0], :])
                    row += blk[3].shape[0]
                part, total = _sb_finish(prepared, sums, vb, total)
                parts.append(part)
                totals.append(total)
            part = jnp.concatenate(parts, axis=0)
            total = jnp.concatenate(totals, axis=0)
            acc_ref[a, h] = part if diagonal else acc_ref[a, h] + part
            tot_ref[a, h] = total
            top = jnp.where(j >= 1, jnp.max(total), DEAD_TOTAL)
            largest = top if largest is None else jnp.maximum(largest, top)
        return largest

    def cond(carry):
        return carry[1] > LOG_WEIGHT_FLOOR

    def body(carry):
        return carry[0] + 1, visit(carry[0], False)

    lax.while_loop(cond, body, (jnp.int32(1), visit(0, True)))
    for a in range(SB_QTILES_PER_STEP):
        for h in range(SB_HEADS_PER_STEP):
            o_ref[a * t:(a + 1) * t, h * SB_HEAD_DIM:(h + 1) * SB_HEAD_DIM] = (
                acc_ref[a, h].astype(o_ref.dtype))


def _sb_attention(qkv, batch, seq):
    t = SB_TILE
    rows = SB_QTILES_PER_STEP * t
    nq = seq // rows
    width = SB_HEADS_PER_STEP * SB_HEAD_DIM
    groups = SB_HEADS // SB_HEADS_PER_STEP
    qspec = pl.BlockSpec((rows, width), lambda b, g, i: (b * nq + i, g))
    kspec = pl.BlockSpec((seq, width), lambda b, g, i: (b, groups + g))
    vspec = pl.BlockSpec((seq, width), lambda b, g, i: (b, 2 * groups + g))
    chains = (SB_QTILES_PER_STEP, SB_HEADS_PER_STEP)
    return pl.pallas_call(
        _sb_kernel,
        out_shape=jax.ShapeDtypeStruct((batch * seq, SB_WIDTH), BF16),
        grid=(batch, groups, nq),
        in_specs=[qspec, kspec, vspec],
        out_specs=qspec,
        scratch_shapes=[pltpu.VMEM(chains + (t, SB_HEAD_DIM), F32),
                        pltpu.VMEM(chains + (t, SB_SUB), F32)],
        compiler_params=_params(3),
        name="sb_attention",
    )(qkv, qkv, qkv)


def _ssd_kernel(xc_ref, sc_ref, dsk_ref, o_ref, st_ref):
    n = SSM_CHUNK
    n_seq = xc_ref.shape[0]

    @pl.when(pl.program_id(0) == 0)
    def _():
        st_ref[...] = jnp.zeros_like(st_ref)

    rows = lax.broadcasted_iota(jnp.int32, (n, n), 0)
    cols = lax.broadcasted_iota(jnp.int32, (n, n), 1)
    causal = rows >= cols
    low = cols < SSM_HEAD_DIM
    low_row = low[0:1, :]
    b_off = SSM_D_INNER
    c_off = SSM_D_INNER + SSM_GROUPS * SSM_STATE

    def scalars(b):
        names = ("cum", "cum_t", "ecum", "dt_t", "w_t")
        return {name: sc_ref[b, :, k * LANES:(k + 1) * LANES] for k, name in enumerate(names)}

    per_seq = [scalars(b) for b in range(n_seq)]
    group_cache = {}

    def group(b, g):
        if (b, g) not in group_cache:
            bg = xc_ref[b, :, b_off + g * SSM_STATE:b_off + (g + 1) * SSM_STATE]
            cg = xc_ref[b, :, c_off + g * SSM_STATE:c_off + (g + 1) * SSM_STATE]
            cb = lax.dot_general(cg, bg, (((1,), (1,)), ((), ())), preferred_element_type=F32)
            group_cache[b, g] = (cb, cg.astype(F32), bg.astype(F32).T)
        return group_cache[b, g]

    for p in range(SSM_HEADS // 2):
        lanes = slice(p * LANES, (p + 1) * LANES)
        for b in range(n_seq):
            s = per_seq[b]
            state = st_ref[b, p]
            xs = xc_ref[b, :, lanes].astype(F32)
            xs_pair = [jnp.where(low, xs, 0.0).astype(BF16), jnp.where(low, 0.0, xs).astype(BF16)]
            st_pair = [jnp.where(low, state, 0.0).astype(BF16),
                       jnp.where(low, 0.0, state).astype(BF16)]
            g_mats, c_mats, b_mats = [], [], []
            for h in (2 * p, 2 * p + 1):
                cb, cg, bg_t = group(b, h // SSM_HEADS_PER_GROUP)
                decay = jnp.where(
                    causal, jnp.exp(s["cum"][:, h:h + 1] - s["cum_t"][h:h + 1, :]), 0.0)
                g_mats.append((cb * decay * s["dt_t"][h:h + 1, :]).astype(BF16))
                c_mats.append((cg * s["ecum"][:, h:h + 1]).astype(BF16))
                b_mats.append((bg_t * s["w_t"][h:h + 1, :]).astype(BF16))
            y = _dot(jnp.concatenate(g_mats + c_mats, axis=1),
                     jnp.concatenate(xs_pair + st_pair, axis=0))
            o_ref[b, :, lanes] = (y + dsk_ref[:, lanes] * xs).astype(o_ref.dtype)
            last = s["ecum"][n - 1:n, :]
            state_decay = jnp.where(low_row, last[:, 2 * p:2 * p + 1],
                                    last[:, 2 * p + 1:2 * p + 2])
            st_ref[b, p] = state * state_decay + _dot(jnp.concatenate(b_mats, axis=1),
                                                      jnp.concatenate(xs_pair, axis=0))


def _ssd(xc, scalars, d_skip, batch, seq):
    n = SSM_CHUNK
    block = lambda w: pl.BlockSpec((batch, n, w), lambda c: (0, c, 0))
    return pl.pallas_call(
        _ssd_kernel,
        out_shape=jax.ShapeDtypeStruct((batch, seq, SSM_D_INNER), BF16),
        grid=(seq // n,),
        in_specs=[block(SSM_CONV_DIM), block(SCAN_SCALARS * LANES), _resident((1, SSM_D_INNER))],
        out_specs=block(SSM_D_INNER),
        scratch_shapes=[pltpu.VMEM((batch, SSM_HEADS // 2, SSM_STATE, LANES), F32)],
        compiler_params=_params(1),
        name="ssd",
    )(xc.reshape(batch, seq, -1), scalars.reshape(batch, seq, -1),
      jnp.repeat(d_skip, SSM_HEAD_DIM).reshape(1, -1)).reshape(batch * seq, SSM_D_INNER)


def _merge_kernel(x_ref, ysb_ref, y_ref, z_ref, gates_ref, nrm_ref, wsb_ref, wssm_ref,
                  wo_ref, o_ref, yn_ref):
    tm = MERGE_TOKEN_TILE
    low = lax.broadcasted_iota(jnp.int32, (tm, LANES), 1) < SSM_HEAD_DIM
    rsum = lambda v: jnp.sum(v, axis=-1, keepdims=True)
    y_sb = _dot(ysb_ref[...], wsb_ref[...])
    y_ssm = None
    pairs_per_part = SSM_GROUPS // 4
    for gp in range(SSM_GROUPS // 2):
        tiles = [slice((3 * gp + i) * LANES, (3 * gp + i + 1) * LANES) for i in range(3)]
        ys = [y_ref[:, s].astype(F32) * _silu(z_ref[:, s].astype(F32)) for s in tiles]
        sq = [v * v for v in ys]
        ss_a = rsum(sq[0]) + rsum(jnp.where(low, sq[1], 0.0))
        ss_b = rsum(sq[2]) + rsum(jnp.where(low, 0.0, sq[1]))
        r_a = lax.rsqrt(ss_a * (1.0 / SSM_GROUP_WIDTH) + NORM_EPS)
        r_b = lax.rsqrt(ss_b * (1.0 / SSM_GROUP_WIDTH) + NORM_EPS)
        scales = [r_a, jnp.where(low, r_a, r_b), r_b]
        for s, v, r in zip(tiles, ys, scales):
            yn_ref[:, s] = (v * r * nrm_ref[:, s]).astype(BF16)
        if (gp + 1) % pairs_per_part == 0:
            part = slice((gp + 1 - pairs_per_part) * 3 * LANES, (gp + 1) * 3 * LANES)
            term = _dot(yn_ref[:, part], wssm_ref[part, :])
            y_ssm = term if y_ssm is None else y_ssm + term
    merged = (jax.nn.sigmoid(gates_ref[:, :D_MODEL].astype(F32)) * y_sb
              + jax.nn.sigmoid(gates_ref[:, D_MODEL:].astype(F32)) * y_ssm)
    o_ref[...] = x_ref[...] + _dot(merged.astype(BF16), wo_ref[...])


def _merge(x, y_sb, y_scan, z, gates, ssm_norm, w_sb, w_ssm, w_out, layer):
    t, d = x.shape
    tm = MERGE_TOKEN_TILE
    row = lambda w: pl.BlockSpec((tm, w), lambda i: (i, 0))
    return pl.pallas_call(
        _merge_kernel,
        out_shape=jax.ShapeDtypeStruct((t, d), F32),
        grid=(t // tm,),
        in_specs=[row(d), row(SB_WIDTH), row(SSM_D_INNER), row(SSM_D_INNER), row(2 * d),
                  _resident((1, SSM_D_INNER)), _layer_resident(w_sb, layer),
                  _layer_resident(w_ssm, layer), _layer_resident(w_out, layer)],
        out_specs=row(d),
        scratch_shapes=[pltpu.VMEM((tm, SSM_D_INNER), BF16)],
        compiler_params=_params(1),
        name="merge",
    )(x, y_sb, y_scan, z, gates, ssm_norm.reshape(1, -1), w_sb, w_ssm, w_out)


def kernel(x, ffn1_norm, ffn1_w_up, ffn1_w_down, mix_norm, w_in, conv_w, conv_b, dt_bias,
           a_log, d_skip, ssm_norm, w_branch_sb, w_branch_ssm, w_out, ffn2_norm, ffn2_w_up,
           ffn2_w_down, final_norm):
    batch, seq, d = x.shape
    depth = w_in.shape[0]
    assert d == D_MODEL and seq % (SB_TILE * SB_QTILES_PER_STEP) == 0
    assert seq % SSM_CHUNK == 0
    assert seq % PROJ_TOKEN_TILE == 0 and (batch * seq) % FFN_TOKEN_TILE == 0
    bf16 = lambda w: w.astype(BF16)
    up1, down1, up2, down2 = map(bf16, (ffn1_w_up, ffn1_w_down, ffn2_w_up, ffn2_w_down))
    w_sb, w_ssm, w_o = map(bf16, (w_branch_sb, w_branch_ssm, w_out))
    dt_off = 3 * SB_WIDTH + SSM_D_INNER + SSM_CONV_DIM
    w_all = bf16(w_in[:, :, :dt_off])
    w_dt = jnp.pad(bf16(w_in[:, :, dt_off:dt_off + SSM_HEADS]),
                   ((0, 0), (0, 0), (0, LANES - SSM_HEADS)))
    w_gates = bf16(w_in[:, :, dt_off + SSM_HEADS:])
    xf = x.reshape(batch * seq, d)
    for l in range(depth):
        xf = _ffn(xf, ffn1_norm[l], up1, down1, l, None)
        qkv, z, xc, scalars, gates = _in_proj(
            xf, mix_norm[l], w_all, w_dt, w_gates, l, conv_w[l], conv_b[l], dt_bias[l],
            a_log[l], seq)
        y_sb = _sb_attention(qkv, batch, seq)
        y_scan = _ssd(xc, scalars, d_skip[l], batch, seq)
        xf = _merge(xf, y_sb, y_scan, z, gates, ssm_norm[l], w_sb, w_ssm, w_o, l)
        xf = _ffn(xf, ffn2_norm[l], up2, down2, l, final_norm if l == depth - 1 else None)
    return xf.reshape(batch, seq, d)
```

```python
import functools

import jax
import jax.numpy as jnp
from jax import lax
from jax.experimental import pallas as pl
from jax.experimental.pallas import tpu as pltpu

F32 = jnp.float32
BF16 = jnp.bfloat16

D_MODEL = 1024
SB_HEADS = 4
SB_HEAD_DIM = 128
SB_WIDTH = SB_HEADS * SB_HEAD_DIM
SSM_D_INNER = 1536
SSM_HEAD_DIM = 64
SSM_HEADS = SSM_D_INNER // SSM_HEAD_DIM
SSM_GROUPS = 8
SSM_HEADS_PER_GROUP = SSM_HEADS // SSM_GROUPS
SSM_STATE = 128
SSM_CONV = 4
SSM_CONV_DIM = SSM_D_INNER + 2 * SSM_GROUPS * SSM_STATE
SSM_GROUP_WIDTH = SSM_D_INNER // SSM_GROUPS
FFN_HIDDEN = 2816
NORM_EPS = 1e-6

LANES = 128
SUBLANES = 8
MXU_DIM = 256
VMEM_LIMIT = 56 * 1024 * 1024

FFN_TOKEN_TILE = 512
FFN_HIDDEN_CHUNKS = (5 * MXU_DIM, 6 * MXU_DIM)
PROJ_TOKEN_TILE = 256
MERGE_TOKEN_TILE = 512
SB_TILE = 256
SB_SUB = 128
SB_HEADS_PER_STEP = 2
SB_QTILES_PER_STEP = 4
LOG_WEIGHT_FLOOR = -105.0
DEAD_TOTAL = -1e30
SSM_CHUNK = 128
SCAN_SCALARS = 5
CONV_COL_TILE = 256
HALO = SUBLANES

assert sum(FFN_HIDDEN_CHUNKS) == FFN_HIDDEN


def _params(n_axes, flags=None):
    return pltpu.CompilerParams(
        dimension_semantics=("arbitrary",) * n_axes, vmem_limit_bytes=VMEM_LIMIT, flags=flags)


def _resident(shape):
    zeros = (0,) * len(shape)
    return pl.BlockSpec(shape, lambda *_: zeros, pipeline_mode=pl.Buffered(1))


def _layer_resident(stacked, layer):
    zeros = (0,) * (stacked.ndim - 1)
    return pl.BlockSpec((None,) + stacked.shape[1:], lambda *_: (layer,) + zeros,
                        pipeline_mode=pl.Buffered(1))


def _rms_norm(x, gain):
    ms = jnp.mean(x * x, axis=-1, keepdims=True)
    return x * lax.rsqrt(ms + NORM_EPS) * gain


def _silu(x):
    return x * jax.nn.sigmoid(x)


def _softplus(x):
    return jnp.maximum(x, 0.0) + jnp.log1p(jnp.exp(-jnp.abs(x)))


def _dot(a, b):
    return jnp.dot(a, b, preferred_element_type=F32)


def _split_bf16(x, parts):
    out = []
    for _ in range(parts - 1):
        hi = x.astype(BF16)
        out.append(hi)
        x = x - hi.astype(F32)
    out.append(x.astype(BF16))
    return out


def _ffn_kernel(x_ref, g_ref, wu_ref, wd_ref, *rest, final_norm):
    if final_norm:
        fg_ref, o_ref = rest
    else:
        (o_ref,) = rest
    x = x_ref[...]
    xn = _rms_norm(x, g_ref[...]).astype(BF16)
    acc, start = None, 0
    for width in FFN_HIDDEN_CHUNKS:
        cols = slice(start, start + width)
        start += width
        gate = _dot(xn, wu_ref[:, cols])
        up = _dot(xn, wu_ref[:, FFN_HIDDEN + cols.start:FFN_HIDDEN + cols.stop])
        act = (_silu(gate) * up).astype(BF16)
        part = _dot(act, wd_ref[cols, :])
        acc = part if acc is None else acc + part
    y = x + 0.5 * acc
    if final_norm:
        y = _rms_norm(y, fg_ref[...])
    o_ref[...] = y


def _ffn(x, gain, w_up, w_down, layer, final_gain):
    t, d = x.shape
    tm = FFN_TOKEN_TILE
    row = pl.BlockSpec((tm, d), lambda i: (i, 0))
    in_specs = [row, _resident((1, d)), _layer_resident(w_up, layer),
                _layer_resident(w_down, layer)]
    args = [x, gain.reshape(1, d), w_up, w_down]
    if final_gain is not None:
        in_specs.append(_resident((1, d)))
        args.append(final_gain.reshape(1, d))
    return pl.pallas_call(
        functools.partial(_ffn_kernel, final_norm=final_gain is not None),
        out_shape=jax.ShapeDtypeStruct((t, d), F32),
        grid=(t // tm,),
        in_specs=in_specs,
        out_specs=row,
        compiler_params=_params(1),
        name="ffn",
    )(*args)


def _in_proj_kernel(x_ref, g_ref, w_ref, wdt_ref, wgate_ref, convw_ref, convb_ref, dtb_ref,
                    alog_ref, qkv_ref, z_ref, xc_ref, sc_ref, gates_ref, xpad_ref, *,
                    tiles_per_seq):
    tm = PROJ_TOKEN_TILE
    n = SSM_CHUNK
    z_off = 3 * SB_WIDTH
    x_off = z_off + SSM_D_INNER

    @pl.when(pl.program_id(0) == 0)
    def _():
        xpad_ref[...] = jnp.zeros_like(xpad_ref)

    xn = _rms_norm(x_ref[...], g_ref[...]).astype(BF16)
    starts_seq = lax.rem(pl.program_id(0), tiles_per_seq) == 0

    def project(src_ref, src_off, width, o_ref, offset=0, scale=None):
        def tile(c):
            y = _dot(xn, src_ref[:, src_off + c:src_off + c + CONV_COL_TILE])
            o_ref[:, offset + c:offset + c + CONV_COL_TILE] = (
                y if scale is None else y * scale).astype(o_ref.dtype)
        return [functools.partial(tile, c) for c in range(0, width, CONV_COL_TILE)]

    def scan_scalars():
        dt_all = _softplus(_dot(xn, wdt_ref[...]) + dtb_ref[...])
        neg_a = -jnp.exp(alog_ref[...])
        incl = (lax.broadcasted_iota(jnp.int32, (n, n), 0)
                >= lax.broadcasted_iota(jnp.int32, (n, n), 1)).astype(BF16)
        incl3 = jnp.concatenate([incl] * 3, axis=1)
        for c in range(tm // n):
            rs = slice(c * n, (c + 1) * n)
            dt = dt_all[rs, :]
            cum = _dot(incl3, jnp.concatenate(_split_bf16(dt * neg_a, 3), axis=0))
            parts = [cum, cum.T, jnp.exp(cum), dt.T, (dt * jnp.exp(cum[n - 1:n, :] - cum)).T]
            for k, part in enumerate(parts):
                sc_ref[rs, k * LANES:(k + 1) * LANES] = part

    plain = ([scan_scalars] + project(w_ref, 0, SB_WIDTH, qkv_ref, 0, SB_HEAD_DIM ** -0.5)
             + project(w_ref, SB_WIDTH, 2 * SB_WIDTH, qkv_ref, SB_WIDTH)
             + project(w_ref, z_off, SSM_D_INNER, z_ref)
             + project(wgate_ref, 0, 2 * D_MODEL, gates_ref))

    def conv_tile(cs, y, tail):
        ext = jnp.concatenate([tail, y], axis=0)
        acc = convb_ref[:, cs]
        for i in range(SSM_CONV - 1):
            shifted = pltpu.roll(ext, SSM_CONV - 1 - i, axis=0)[HALO:, :]
            acc = acc + convw_ref[i:i + 1, cs] * shifted
        acc = acc + convw_ref[SSM_CONV - 1:SSM_CONV, cs] * y
        xc_ref[:, cs] = _silu(acc).astype(BF16)

    n_tiles = SSM_CONV_DIM // CONV_COL_TILE
    pending = None
    for ct in range(n_tiles):
        cs = slice(ct * CONV_COL_TILE, (ct + 1) * CONV_COL_TILE)
        y = _dot(xn, w_ref[:, x_off + cs.start:x_off + cs.stop])
        tail = jnp.where(starts_seq, 0.0, xpad_ref[:, cs])
        xpad_ref[:, cs] = y[tm - HALO:tm, :]
        for task in plain[ct::n_tiles]:
            task()
        if pending is not None:
            conv_tile(*pending)
        pending = (cs, y, tail)
    conv_tile(*pending)


def _in_proj(x, gain, w_in, w_dt, w_gates, layer, conv_w, conv_b, dt_bias, a_log, seq):
    t, d = x.shape
    tm = PROJ_TOKEN_TILE
    pad = lambda v: jnp.pad(v, (0, LANES - SSM_HEADS)).reshape(1, LANES)
    out_widths = (3 * SB_WIDTH, SSM_D_INNER, SSM_CONV_DIM, SCAN_SCALARS * LANES, 2 * D_MODEL)
    out_dtypes = (BF16, BF16, BF16, F32, BF16)
    row = lambda w: pl.BlockSpec((tm, w), lambda i: (i, 0))
    return pl.pallas_call(
        functools.partial(_in_proj_kernel, tiles_per_seq=seq // tm),
        out_shape=[jax.ShapeDtypeStruct((t, w), dt) for w, dt in zip(out_widths, out_dtypes)],
        grid=(t // tm,),
        in_specs=[row(d), _resident((1, d)), _layer_resident(w_in, layer),
                  _layer_resident(w_dt, layer), _layer_resident(w_gates, layer),
                  _resident((SSM_CONV, SSM_CONV_DIM)), _resident((1, SSM_CONV_DIM)),
                  _resident((1, LANES)), _resident((1, LANES))],
        out_specs=[row(w) for w in out_widths],
        scratch_shapes=[pltpu.VMEM((HALO, SSM_CONV_DIM), F32)],
        compiler_params=_params(1),
        name="in_proj",
    )(x, gain.reshape(1, d), w_in, w_dt, w_gates, conv_w, conv_b.reshape(1, -1), pad(dt_bias),
      pad(a_log))


def _sb_prepare(z, blocks):
    s = SB_SUB
    out = []
    for b, mask in blocks:
        zb = z[:, b * s:(b + 1) * s]
        tail = jnp.log(1.0 + jnp.exp(-jnp.abs(zb)))
        log_keep = -(jnp.maximum(zb, 0.0) + tail)
        if mask is not None:
            log_keep = jnp.where(mask, log_keep, 0.0)
        out.append((b, mask, jnp.minimum(zb, 0.0) - tail,
                    jnp.concatenate(_split_bf16(log_keep, 2), axis=1)))
    return out


def _sb_finish(prepared, sums, vb, total):
    s = SB_SUB
    weights = {}
    for (b, mask, log_beta, _), block_sums in zip(prepared, sums):
        w = jnp.exp(log_beta + block_sums[:, :s] + total)
        if mask is not None:
            w = jnp.where(mask, w, 0.0)
        weights[b] = w.astype(BF16)
        total = total + block_sums[:, s:]
    order = sorted(weights)
    w_all = jnp.concatenate([weights[b] for b in order], axis=1)
    return _dot(w_all, vb[order[0] * s:(order[-1] + 1) * s, :]), total


def _sb_kernel(q_ref, k_ref, v_ref, o_ref, acc_ref, tot_ref):
    i = pl.program_id(2)
    t, s = SB_TILE, SB_SUB
    rows = lax.broadcasted_iota(jnp.int32, (s, s), 0)
    cols = lax.broadcasted_iota(jnp.int32, (s, s), 1)
    later = (rows > cols).astype(BF16)
    half = jnp.concatenate([later, jnp.ones((s, s), BF16)], axis=1)
    sum_mat = jnp.concatenate([half, half], axis=0)
    before = cols < rows

    def visit(back, diagonal):
        work = []
        for a in range(SB_QTILES_PER_STEP):
            rs = slice(a * t, (a + 1) * t)
            j = i * SB_QTILES_PER_STEP + a - back
            start = pl.multiple_of(jnp.maximum(j, 0) * t, t)
            for h in range(SB_HEADS_PER_STEP):
                hs = slice(h * SB_HEAD_DIM, (h + 1) * SB_HEAD_DIM)
                z = lax.dot_general(q_ref[rs, hs], k_ref[pl.ds(start, t), hs],
                                    (((1,), (1,)), ((), ())), preferred_element_type=F32)
                if diagonal:
                    zero = jnp.zeros((s, s), F32)
                    units = [(_sb_prepare(z[:s], [(0, before)]), zero),
                             (_sb_prepare(z[s:], [(1, before), (0, None)]), zero)]
                else:
                    total = jnp.where(j >= 0, tot_ref[a, h], DEAD_TOTAL)
                    units = [(_sb_prepare(z, [(1, None), (0, None)]), total)]
                work.append((a, h, j, v_ref[pl.ds(start, t), hs], units))

        operands = [blk[3] for _, _, _, _, units in work for prep, _ in units for blk in prep]
        all_sums = _dot(jnp.concatenate(operands, axis=0), sum_mat)

        largest, row = None, 0
        for a, h, j, vb, units in work:
            parts, totals = [], []
            for prepared, total in units:
                sums = []
                for blk in prepared:
                    sums.append(all_sums[row:row + blk[3].shape[0], :])
                    row += blk[3].shape[0]
                part, total = _sb_finish(prepared, sums, vb, total)
                parts.append(part)
                totals.append(total)
            part = jnp.concatenate(parts, axis=0)
            total = jnp.concatenate(totals, axis=0)
            acc_ref[a, h] = part if diagonal else acc_ref[a, h] + part
            tot_ref[a, h] = total
            top = jnp.where(j >= 1, jnp.max(total), DEAD_TOTAL)
            largest = top if largest is None else jnp.maximum(largest, top)
        return largest

    def cond(carry):
        return carry[1] > LOG_WEIGHT_FLOOR

    def body(carry):
        return carry[0] + 1, visit(carry[0], False)

    lax.while_loop(cond, body, (jnp.int32(1), visit(0, True)))
    for a in range(SB_QTILES_PER_STEP):
        for h in range(SB_HEADS_PER_STEP):
            o_ref[a * t:(a + 1) * t, h * SB_HEAD_DIM:(h + 1) * SB_HEAD_DIM] = (
                acc_ref[a, h].astype(o_ref.dtype))


def _sb_attention(qkv, batch, seq):
    t = SB_TILE
    rows = SB_QTILES_PER_STEP * t
    nq = seq // rows
    width = SB_HEADS_PER_STEP * SB_HEAD_DIM
    groups = SB_HEADS // SB_HEADS_PER_STEP
    qspec = pl.BlockSpec((rows, width), lambda b, g, i: (b * nq + i, g))
    kspec = pl.BlockSpec((seq, width), lambda b, g, i: (b, groups + g))
    vspec = pl.BlockSpec((seq, width), lambda b, g, i: (b, 2 * groups + g))
    chains = (SB_QTILES_PER_STEP, SB_HEADS_PER_STEP)
    return pl.pallas_call(
        _sb_kernel,
        out_shape=jax.ShapeDtypeStruct((batch * seq, SB_WIDTH), BF16),
        grid=(batch, groups, nq),
        in_specs=[qspec, kspec, vspec],
        out_specs=qspec,
        scratch_shapes=[pltpu.VMEM(chains + (t, SB_HEAD_DIM), F32),
                        pltpu.VMEM(chains + (t, SB_SUB), F32)],
        compiler_params=_params(3),
        name="sb_attention",
    )(qkv, qkv, qkv)


def _ssd_kernel(xc_ref, sc_ref, dsk_ref, o_ref, st_ref):
    n = SSM_CHUNK
    n_seq = xc_ref.shape[0]

    @pl.when(pl.program_id(0) == 0)
    def _():
        st_ref[...] = jnp.zeros_like(st_ref)

    rows = lax.broadcasted_iota(jnp.int32, (n, n), 0)
    cols = lax.broadcasted_iota(jnp.int32, (n, n), 1)
    causal = rows >= cols
    low = cols < SSM_HEAD_DIM
    low_row = low[0:1, :]
    b_off = SSM_D_INNER
    c_off = SSM_D_INNER + SSM_GROUPS * SSM_STATE

    def scalars(b):
        names = ("cum", "cum_t", "ecum", "dt_t", "w_t")
        return {name: sc_ref[b, :, k * LANES:(k + 1) * LANES] for k, name in enumerate(names)}

    per_seq = [scalars(b) for b in range(n_seq)]
    group_cache = {}

    def group(b, g):
        if (b, g) not in group_cache:
            bg = xc_ref[b, :, b_off + g * SSM_STATE:b_off + (g + 1) * SSM_STATE]
            cg = xc_ref[b, :, c_off + g * SSM_STATE:c_off + (g + 1) * SSM_STATE]
            cb = lax.dot_general(cg, bg, (((1,), (1,)), ((), ())), preferred_element_type=F32)
            group_cache[b, g] = (cb, cg.astype(F32), bg.astype(F32).T)
        return group_cache[b, g]

    for p in range(SSM_HEADS // 2):
        lanes = slice(p * LANES, (p + 1) * LANES)
        for b in range(n_seq):
            s = per_seq[b]
            state = st_ref[b, p]
            xs = xc_ref[b, :, lanes].astype(F32)
            xs_pair = [jnp.where(low, xs, 0.0).astype(BF16), jnp.where(low, 0.0, xs).astype(BF16)]
            st_pair = [jnp.where(low, state, 0.0).astype(BF16),
                       jnp.where(low, 0.0, state).astype(BF16)]
            g_mats, c_mats, b_mats = [], [], []
            for h in (2 * p, 2 * p + 1):
                cb, cg, bg_t = group(b, h // SSM_HEADS_PER_GROUP)
                decay = jnp.where(
                    causal, jnp.exp(s["cum"][:, h:h + 1] - s["cum_t"][h:h + 1, :]), 0.0)
                g_mats.append((cb * decay * s["dt_t"][h:h + 1, :]).astype(BF16))
                c_mats.append((cg * s["ecum"][:, h:h + 1]).astype(BF16))
                b_mats.append((bg_t * s["w_t"][h:h + 1, :]).astype(BF16))
            y = _dot(jnp.concatenate(g_mats + c_mats, axis=1),
                     jnp.concatenate(xs_pair + st_pair, axis=0))
            o_ref[b, :, lanes] = (y + dsk_ref[:, lanes] * xs).astype(o_ref.dtype)
            last = s["ecum"][n - 1:n, :]
            state_decay = jnp.where(low_row, last[:, 2 * p:2 * p + 1],
                                    last[:, 2 * p + 1:2 * p + 2])
            st_ref[b, p] = state * state_decay + _dot(jnp.concatenate(b_mats, axis=1),
                                                      jnp.concatenate(xs_pair, axis=0))


def _ssd(xc, scalars, d_skip, batch, seq):
    n = SSM_CHUNK
    block = lambda w: pl.BlockSpec((batch, n, w), lambda c: (0, c, 0))
    return pl.pallas_call(
        _ssd_kernel,
        out_shape=jax.ShapeDtypeStruct((batch, seq, SSM_D_INNER), BF16),
        grid=(seq // n,),
        in_specs=[block(SSM_CONV_DIM), block(SCAN_SCALARS * LANES), _resident((1, SSM_D_INNER))],
        out_specs=block(SSM_D_INNER),
        scratch_shapes=[pltpu.VMEM((batch, SSM_HEADS // 2, SSM_STATE, LANES), F32)],
        compiler_params=_params(1),
        name="ssd",
    )(xc.reshape(batch, seq, -1), scalars.reshape(batch, seq, -1),
      jnp.repeat(d_skip, SSM_HEAD_DIM).reshape(1, -1)).reshape(batch * seq, SSM_D_INNER)


def _merge_kernel(x_ref, ysb_ref, y_ref, z_ref, gates_ref, nrm_ref, wsb_ref, wssm_ref,
                  wo_ref, o_ref, yn_ref):
    tm = MERGE_TOKEN_TILE
    low = lax.broadcasted_iota(jnp.int32, (tm, LANES), 1) < SSM_HEAD_DIM
    rsum = lambda v: jnp.sum(v, axis=-1, keepdims=True)
    y_sb = _dot(ysb_ref[...], wsb_ref[...])
    y_ssm = None
    pairs_per_part = SSM_GROUPS // 4
    for gp in range(SSM_GROUPS // 2):
        tiles = [slice((3 * gp + i) * LANES, (3 * gp + i + 1) * LANES) for i in range(3)]
        ys = [y_ref[:, s].astype(F32) * _silu(z_ref[:, s].astype(F32)) for s in tiles]
        sq = [v * v for v in ys]
        ss_a = rsum(sq[0]) + rsum(jnp.where(low, sq[1], 0.0))
        ss_b = rsum(sq[2]) + rsum(jnp.where(low, 0.0, sq[1]))
        r_a = lax.rsqrt(ss_a * (1.0 / SSM_GROUP_WIDTH) + NORM_EPS)
        r_b = lax.rsqrt(ss_b * (1.0 / SSM_GROUP_WIDTH) + NORM_EPS)
        scales = [r_a, jnp.where(low, r_a, r_b), r_b]
        for s, v, r in zip(tiles, ys, scales):
            yn_ref[:, s] = (v * r * nrm_ref[:, s]).astype(BF16)
        if (gp + 1) % pairs_per_part == 0:
            part = slice((gp + 1 - pairs_per_part) * 3 * LANES, (gp + 1) * 3 * LANES)
            term = _dot(yn_ref[:, part], wssm_ref[part, :])
            y_ssm = term if y_ssm is None else y_ssm + term
    merged = (jax.nn.sigmoid(gates_ref[:, :D_MODEL].astype(F32)) * y_sb
              + jax.nn.sigmoid(gates_ref[:, D_MODEL:].astype(F32)) * y_ssm)
    o_ref[...] = x_ref[...] + _dot(merged.astype(BF16), wo_ref[...])


def _merge(x, y_sb, y_scan, z, gates, ssm_norm, w_sb, w_ssm, w_out, layer):
    t, d = x.shape
    tm = MERGE_TOKEN_TILE
    row = lambda w: pl.BlockSpec((tm, w), lambda i: (i, 0))
    return pl.pallas_call(
        _merge_kernel,
        out_shape=jax.ShapeDtypeStruct((t, d), F32),
        grid=(t // tm,),
        in_specs=[row(d), row(SB_WIDTH), row(SSM_D_INNER), row(SSM_D_INNER), row(2 * d),
                  _resident((1, SSM_D_INNER)), _layer_resident(w_sb, layer),
                  _layer_resident(w_ssm, layer), _layer_resident(w_out, layer)],
        out_specs=row(d),
        scratch_shapes=[pltpu.VMEM((tm, SSM_D_INNER), BF16)],
        compiler_params=_params(1),
        name="merge",
    )(x, y_sb, y_scan, z, gates, ssm_norm.reshape(1, -1), w_sb, w_ssm, w_out)


def kernel(x, ffn1_norm, ffn1_w_up, ffn1_w_down, mix_norm, w_in, conv_w, conv_b, dt_bias,
           a_log, d_skip, ssm_norm, w_branch_sb, w_branch_ssm, w_out, ffn2_norm, ffn2_w_up,
           ffn2_w_down, final_norm):
    batch, seq, d = x.shape
    depth = w_in.shape[0]
    assert d == D_MODEL and seq % (SB_TILE * SB_QTILES_PER_STEP) == 0
    assert seq % SSM_CHUNK == 0
    assert seq % PROJ_TOKEN_TILE == 0 and (batch * seq) % FFN_TOKEN_TILE == 0
    bf16 = lambda w: w.astype(BF16)
    up1, down1, up2, down2 = map(bf16, (ffn1_w_up, ffn1_w_down, ffn2_w_up, ffn2_w_down))
    w_sb, w_ssm, w_o, w_all = map(bf16, (w_branch_sb, w_branch_ssm, w_out, w_in))
    dt_off = 3 * SB_WIDTH + SSM_D_INNER + SSM_CONV_DIM
    w_dt = jnp.pad(w_all[:, :, dt_off:dt_off + SSM_HEADS],
                   ((0, 0), (0, 0), (0, LANES - SSM_HEADS)))
    w_gates = w_all[:, :, dt_off + SSM_HEADS:]
    xf = x.reshape(batch * seq, d)
    for l in range(depth):
        xf = _ffn(xf, ffn1_norm[l], up1, down1, l, None)
        qkv, z, xc, scalars, gates = _in_proj(
            xf, mix_norm[l], w_all, w_dt, w_gates, l, conv_w[l], conv_b[l], dt_bias[l],
            a_log[l], seq)
        y_sb = _sb_attention(qkv, batch, seq)
        y_scan = _ssd(xc, scalars, d_skip[l], batch, seq)
        xf = _merge(xf, y_sb, y_scan, z, gates, ssm_norm[l], w_sb, w_ssm, w_o, l)
        xf = _ffn(xf, ffn2_norm[l], up2, down2, l, final_norm if l == depth - 1 else None)
    return xf.reshape(batch, seq, d)
```

```python
import functools

import jax
import jax.numpy as jnp
from jax import lax
from jax.experimental import pallas as pl
from jax.experimental.pallas import tpu as pltpu

F32 = jnp.float32
BF16 = jnp.bfloat16

D_MODEL = 1024
SB_HEADS = 4
SB_HEAD_DIM = 128
SB_WIDTH = SB_HEADS * SB_HEAD_DIM
SSM_D_INNER = 1536
SSM_HEAD_DIM = 64
SSM_HEADS = SSM_D_INNER // SSM_HEAD_DIM
SSM_GROUPS = 8
SSM_HEADS_PER_GROUP = SSM_HEADS // SSM_GROUPS
SSM_STATE = 128
SSM_CONV = 4
SSM_CONV_DIM = SSM_D_INNER + 2 * SSM_GROUPS * SSM_STATE
SSM_GROUP_WIDTH = SSM_D_INNER // SSM_GROUPS
FFN_HIDDEN = 2816
NORM_EPS = 1e-6

LANES = 128
SUBLANES = 8
MXU_DIM = 256
VMEM_LIMIT = 56 * 1024 * 1024

FFN_TOKEN_TILE = 512
FFN_HIDDEN_CHUNKS = (5 * MXU_DIM, 6 * MXU_DIM)
PROJ_TOKEN_TILE = 256
MERGE_TOKEN_TILE = 256
SB_TILE = 256
SB_SUB = 128
SB_HEADS_PER_STEP = 2
SB_QTILES_PER_STEP = 4
LOG_WEIGHT_FLOOR = -105.0
DEAD_TOTAL = -1e30
SSM_CHUNK = 128
SCAN_SCALARS = 5
CONV_COL_TILE = 256
HALO = SUBLANES

assert sum(FFN_HIDDEN_CHUNKS) == FFN_HIDDEN


def _params(n_axes, flags=None):
    return pltpu.CompilerParams(
        dimension_semantics=("arbitrary",) * n_axes, vmem_limit_bytes=VMEM_LIMIT, flags=flags)


def _resident(shape):
    zeros = (0,) * len(shape)
    return pl.BlockSpec(shape, lambda *_: zeros, pipeline_mode=pl.Buffered(1))


def _layer_resident(stacked, layer):
    zeros = (0,) * (stacked.ndim - 1)
    return pl.BlockSpec((None,) + stacked.shape[1:], lambda *_: (layer,) + zeros,
                        pipeline_mode=pl.Buffered(1))


def _rms_norm(x, gain):
    ms = jnp.mean(x * x, axis=-1, keepdims=True)
    return x * lax.rsqrt(ms + NORM_EPS) * gain


def _silu(x):
    return x * jax.nn.sigmoid(x)


def _softplus(x):
    return jnp.maximum(x, 0.0) + jnp.log1p(jnp.exp(-jnp.abs(x)))


def _dot(a, b):
    return jnp.dot(a, b, preferred_element_type=F32)


def _split_bf16(x, parts):
    out = []
    for _ in range(parts - 1):
        hi = x.astype(BF16)
        out.append(hi)
        x = x - hi.astype(F32)
    out.append(x.astype(BF16))
    return out


def _ffn_kernel(x_ref, g_ref, wu_ref, wd_ref, *rest, final_norm):
    if final_norm:
        fg_ref, o_ref = rest
    else:
        fg_ref, (o_ref,) = None, rest
    o_ref[...] = _ffn_math(x_ref[...], g_ref, wu_ref, wd_ref, fg_ref)


def _ffn_math(x, g_ref, wu_ref, wd_ref, fg_ref):
    final_norm = fg_ref is not None
    xn = _rms_norm(x, g_ref[...]).astype(BF16)
    acc, start = None, 0
    for width in FFN_HIDDEN_CHUNKS:
        cols = slice(start, start + width)
        start += width
        gate = _dot(xn, wu_ref[:, cols])
        up = _dot(xn, wu_ref[:, FFN_HIDDEN + cols.start:FFN_HIDDEN + cols.stop])
        act = (_silu(gate) * up).astype(BF16)
        part = _dot(act, wd_ref[cols, :])
        acc = part if acc is None else acc + part
    y = x + 0.5 * acc
    if final_norm:
        y = _rms_norm(y, fg_ref[...])
    return y


def _ffn(x, gain, w_up, w_down, layer, final_gain):
    t, d = x.shape
    tm = FFN_TOKEN_TILE
    row = pl.BlockSpec((tm, d), lambda i: (i, 0))
    in_specs = [row, _resident((1, d)), _layer_resident(w_up, layer),
                _layer_resident(w_down, layer)]
    args = [x, gain.reshape(1, d), w_up, w_down]
    if final_gain is not None:
        in_specs.append(_resident((1, d)))
        args.append(final_gain.reshape(1, d))
    return pl.pallas_call(
        functools.partial(_ffn_kernel, final_norm=final_gain is not None),
        out_shape=jax.ShapeDtypeStruct((t, d), F32),
        grid=(t // tm,),
        in_specs=in_specs,
        out_specs=row,
        compiler_params=_params(1),
        name="ffn",
    )(*args)


def _in_proj_kernel(x_ref, g_ref, w_ref, wdt_ref, wgate_ref, convw_ref, convb_ref, dtb_ref,
                    alog_ref, qkv_ref, z_ref, xc_ref, sc_ref, gates_ref, xpad_ref, *,
                    tiles_per_seq):
    tm = PROJ_TOKEN_TILE
    n = SSM_CHUNK
    z_off = 3 * SB_WIDTH
    x_off = z_off + SSM_D_INNER

    @pl.when(pl.program_id(0) == 0)
    def _():
        xpad_ref[...] = jnp.zeros_like(xpad_ref)

    xn = _rms_norm(x_ref[...], g_ref[...]).astype(BF16)
    starts_seq = lax.rem(pl.program_id(0), tiles_per_seq) == 0

    def project(src_ref, src_off, width, o_ref, offset=0, scale=None):
        def tile(c):
            y = _dot(xn, src_ref[:, src_off + c:src_off + c + CONV_COL_TILE])
            o_ref[:, offset + c:offset + c + CONV_COL_TILE] = (
                y if scale is None else y * scale).astype(o_ref.dtype)
        return [functools.partial(tile, c) for c in range(0, width, CONV_COL_TILE)]

    def scan_scalars():
        dt_all = _softplus(_dot(xn, wdt_ref[...]) + dtb_ref[...])
        neg_a = -jnp.exp(alog_ref[...])
        incl = (lax.broadcasted_iota(jnp.int32, (n, n), 0)
                >= lax.broadcasted_iota(jnp.int32, (n, n), 1)).astype(BF16)
        incl3 = jnp.concatenate([incl] * 3, axis=1)
        for c in range(tm // n):
            rs = slice(c * n, (c + 1) * n)
            dt = dt_all[rs, :]
            cum = _dot(incl3, jnp.concatenate(_split_bf16(dt * neg_a, 3), axis=0))
            parts = [cum, cum.T, jnp.exp(cum), dt.T, (dt * jnp.exp(cum[n - 1:n, :] - cum)).T]
            for k, part in enumerate(parts):
                sc_ref[rs, k * LANES:(k + 1) * LANES] = part

    plain = ([scan_scalars] + project(w_ref, 0, SB_WIDTH, qkv_ref, 0, SB_HEAD_DIM ** -0.5)
             + project(w_ref, SB_WIDTH, 2 * SB_WIDTH, qkv_ref, SB_WIDTH)
             + project(w_ref, z_off, SSM_D_INNER, z_ref)
             + project(wgate_ref, 0, 2 * D_MODEL, gates_ref))

    def conv_tile(cs, y, tail):
        ext = jnp.concatenate([tail, y], axis=0)
        acc = convb_ref[:, cs]
        for i in range(SSM_CONV - 1):
            shifted = pltpu.roll(ext, SSM_CONV - 1 - i, axis=0)[HALO:, :]
            acc = acc + convw_ref[i:i + 1, cs] * shifted
        acc = acc + convw_ref[SSM_CONV - 1:SSM_CONV, cs] * y
        xc_ref[:, cs] = _silu(acc).astype(BF16)

    n_tiles = SSM_CONV_DIM // CONV_COL_TILE
    pending = None
    for ct in range(n_tiles):
        cs = slice(ct * CONV_COL_TILE, (ct + 1) * CONV_COL_TILE)
        y = _dot(xn, w_ref[:, x_off + cs.start:x_off + cs.stop])
        tail = jnp.where(starts_seq, 0.0, xpad_ref[:, cs])
        xpad_ref[:, cs] = y[tm - HALO:tm, :]
        for task in plain[ct::n_tiles]:
            task()
        if pending is not None:
            conv_tile(*pending)
        pending = (cs, y, tail)
    conv_tile(*pending)


def _in_proj(x, gain, w_in, w_dt, w_gates, layer, conv_w, conv_b, dt_bias, a_log, seq):
    t, d = x.shape
    tm = PROJ_TOKEN_TILE
    pad = lambda v: jnp.pad(v, (0, LANES - SSM_HEADS)).reshape(1, LANES)
    out_widths = (3 * SB_WIDTH, SSM_D_INNER, SSM_CONV_DIM, SCAN_SCALARS * LANES, 2 * D_MODEL)
    out_dtypes = (BF16, BF16, BF16, F32, BF16)
    row = lambda w: pl.BlockSpec((tm, w), lambda i: (i, 0))
    return pl.pallas_call(
        functools.partial(_in_proj_kernel, tiles_per_seq=seq // tm),
        out_shape=[jax.ShapeDtypeStruct((t, w), dt) for w, dt in zip(out_widths, out_dtypes)],
        grid=(t // tm,),
        in_specs=[row(d), _resident((1, d)), _layer_resident(w_in, layer),
                  _layer_resident(w_dt, layer), _layer_resident(w_gates, layer),
                  _resident((SSM_CONV, SSM_CONV_DIM)), _resident((1, SSM_CONV_DIM)),
                  _resident((1, LANES)), _resident((1, LANES))],
        out_specs=[row(w) for w in out_widths],
        scratch_shapes=[pltpu.VMEM((HALO, SSM_CONV_DIM), F32)],
        compiler_params=_params(1),
        name="in_proj",
    )(x, gain.reshape(1, d), w_in, w_dt, w_gates, conv_w, conv_b.reshape(1, -1), pad(dt_bias),
      pad(a_log))


def _sb_prepare(z, blocks):
    s = SB_SUB
    out = []
    for b, mask in blocks:
        zb = z[:, b * s:(b + 1) * s]
        tail = jnp.log(1.0 + jnp.exp(-jnp.abs(zb)))
        log_keep = -(jnp.maximum(zb, 0.0) + tail)
        if mask is not None:
            log_keep = jnp.where(mask, log_keep, 0.0)
        out.append((b, mask, jnp.minimum(zb, 0.0) - tail,
                    jnp.concatenate(_split_bf16(log_keep, 2), axis=1)))
    return out


def _sb_finish(prepared, sums, vb, total):
    s = SB_SUB
    weights = {}
    for (b, mask, log_beta, _), block_sums in zip(prepared, sums):
        w = jnp.exp(log_beta + block_sums[:, :s] + total)
        if mask is not None:
            w = jnp.where(mask, w, 0.0)
        weights[b] = w.astype(BF16)
        total = total + block_sums[:, s:]
    order = sorted(weights)
    w_all = jnp.concatenate([weights[b] for b in order], axis=1)
    return _dot(w_all, vb[order[0] * s:(order[-1] + 1) * s, :]), total


def _sb_kernel(q_ref, k_ref, v_ref, o_ref, acc_ref, tot_ref):
    i = pl.program_id(2)
    t, s = SB_TILE, SB_SUB
    rows = lax.broadcasted_iota(jnp.int32, (s, s), 0)
    cols = lax.broadcasted_iota(jnp.int32, (s, s), 1)
    later = (rows > cols).astype(BF16)
    half = jnp.concatenate([later, jnp.ones((s, s), BF16)], axis=1)
    sum_mat = jnp.concatenate([half, half], axis=0)
    before = cols < rows

    def visit(back, diagonal):
        work = []
        for a in range(SB_QTILES_PER_STEP):
            rs = slice(a * t, (a + 1) * t)
            j = i * SB_QTILES_PER_STEP + a - back
            start = pl.multiple_of(jnp.maximum(j, 0) * t, t)
            for h in range(SB_HEADS_PER_STEP):
                hs = slice(h * SB_HEAD_DIM, (h + 1) * SB_HEAD_DIM)
                z = lax.dot_general(q_ref[rs, hs], k_ref[pl.ds(start, t), hs],
                                    (((1,), (1,)), ((), ())), preferred_element_type=F32)
                if diagonal:
                    zero = jnp.zeros((s, s), F32)
                    units = [(_sb_prepare(z[:s], [(0, before)]), zero),
                             (_sb_prepare(z[s:], [(1, before), (0, None)]), zero)]
                else:
                    total = jnp.where(j >= 0, tot_ref[a, h], DEAD_TOTAL)
                    units = [(_sb_prepare(z, [(1, None), (0, None)]), total)]
                work.append((a, h, j, v_ref[pl.ds(start, t), hs], units))

        operands = [blk[3] for _, _, _, _, units in work for prep, _ in units for blk in prep]
        all_sums = _dot(jnp.concatenate(operands, axis=0), sum_mat)

        largest, row = None, 0
        for a, h, j, vb, units in work:
            parts, totals = [], []
            for prepared, total in units:
                sums = []
                for blk in prepared:
                    sums.append(all_sums[row:row + blk[3].shape[0], :])
                    row += blk[3].shape[0]
                part, total = _sb_finish(prepared, sums, vb, total)
                parts.append(part)
                totals.append(total)
            part = jnp.concatenate(parts, axis=0)
            total = jnp.concatenate(totals, axis=0)
            acc_ref[a, h] = part if diagonal else acc_ref[a, h] + part
            tot_ref[a, h] = total
            top = jnp.where(j >= 1, jnp.max(total), DEAD_TOTAL)
            largest = top if largest is None else jnp.maximum(largest, top)
        return largest

    def cond(carry):
        return carry[1] > LOG_WEIGHT_FLOOR

    def body(carry):
        return carry[0] + 1, visit(carry[0], False)

    lax.while_loop(cond, body, (jnp.int32(1), visit(0, True)))
    for a in range(SB_QTILES_PER_STEP):
        for h in range(SB_HEADS_PER_STEP):
            o_ref[a * t:(a + 1) * t, h * SB_HEAD_DIM:(h + 1) * SB_HEAD_DIM] = (
                acc_ref[a, h].astype(o_ref.dtype))


def _sb_attention(qkv, batch, seq):
    t = SB_TILE
    rows = SB_QTILES_PER_STEP * t
    nq = seq // rows
    width = SB_HEADS_PER_STEP * SB_HEAD_DIM
    groups = SB_HEADS // SB_HEADS_PER_STEP
    qspec = pl.BlockSpec((rows, width), lambda b, g, i: (b * nq + i, g))
    kspec = pl.BlockSpec((seq, width), lambda b, g, i: (b, groups + g))
    vspec = pl.BlockSpec((seq, width), lambda b, g, i: (b, 2 * groups + g))
    chains = (SB_QTILES_PER_STEP, SB_HEADS_PER_STEP)
    return pl.pallas_call(
        _sb_kernel,
        out_shape=jax.ShapeDtypeStruct((batch * seq, SB_WIDTH), BF16),
        grid=(batch, groups, nq),
        in_specs=[qspec, kspec, vspec],
        out_specs=qspec,
        scratch_shapes=[pltpu.VMEM(chains + (t, SB_HEAD_DIM), F32),
                        pltpu.VMEM(chains + (t, SB_SUB), F32)],
        compiler_params=_params(3),
        name="sb_attention",
    )(qkv, qkv, qkv)


def _ssd_kernel(xc_ref, sc_ref, dsk_ref, o_ref, st_ref):
    n = SSM_CHUNK
    n_seq = xc_ref.shape[0]

    @pl.when(pl.program_id(0) == 0)
    def _():
        st_ref[...] = jnp.zeros_like(st_ref)

    rows = lax.broadcasted_iota(jnp.int32, (n, n), 0)
    cols = lax.broadcasted_iota(jnp.int32, (n, n), 1)
    causal = rows >= cols
    low = cols < SSM_HEAD_DIM
    low_row = low[0:1, :]
    b_off = SSM_D_INNER
    c_off = SSM_D_INNER + SSM_GROUPS * SSM_STATE

    def scalars(b):
        names = ("cum", "cum_t", "ecum", "dt_t", "w_t")
        return {name: sc_ref[b, :, k * LANES:(k + 1) * LANES] for k, name in enumerate(names)}

    per_seq = [scalars(b) for b in range(n_seq)]
    group_cache = {}

    def group(b, g):
        if (b, g) not in group_cache:
            bg = xc_ref[b, :, b_off + g * SSM_STATE:b_off + (g + 1) * SSM_STATE]
            cg = xc_ref[b, :, c_off + g * SSM_STATE:c_off + (g + 1) * SSM_STATE]
            cb = lax.dot_general(cg, bg, (((1,), (1,)), ((), ())), preferred_element_type=F32)
            group_cache[b, g] = (cb, cg.astype(F32), bg.astype(F32).T)
        return group_cache[b, g]

    for p in range(SSM_HEADS // 2):
        lanes = slice(p * LANES, (p + 1) * LANES)
        for b in range(n_seq):
            s = per_seq[b]
            state = st_ref[b, p]
            xs = xc_ref[b, :, lanes].astype(F32)
            xs_pair = [jnp.where(low, xs, 0.0).astype(BF16), jnp.where(low, 0.0, xs).astype(BF16)]
            st_pair = [jnp.where(low, state, 0.0).astype(BF16),
                       jnp.where(low, 0.0, state).astype(BF16)]
            g_mats, c_mats, b_mats = [], [], []
            for h in (2 * p, 2 * p + 1):
                cb, cg, bg_t = group(b, h // SSM_HEADS_PER_GROUP)
                decay = jnp.where(
                    causal, jnp.exp(s["cum"][:, h:h + 1] - s["cum_t"][h:h + 1, :]), 0.0)
                g_mats.append((cb * decay * s["dt_t"][h:h + 1, :]).astype(BF16))
                c_mats.append((cg * s["ecum"][:, h:h + 1]).astype(BF16))
                b_mats.append((bg_t * s["w_t"][h:h + 1, :]).astype(BF16))
            y = _dot(jnp.concatenate(g_mats + c_mats, axis=1),
                     jnp.concatenate(xs_pair + st_pair, axis=0))
            o_ref[b, :, lanes] = (y + dsk_ref[:, lanes] * xs).astype(o_ref.dtype)
            last = s["ecum"][n - 1:n, :]
            state_decay = jnp.where(low_row, last[:, 2 * p:2 * p + 1],
                                    last[:, 2 * p + 1:2 * p + 2])
            st_ref[b, p] = state * state_decay + _dot(jnp.concatenate(b_mats, axis=1),
                                                      jnp.concatenate(xs_pair, axis=0))


def _ssd(xc, scalars, d_skip, batch, seq):
    n = SSM_CHUNK
    block = lambda w: pl.BlockSpec((batch, n, w), lambda c: (0, c, 0))
    return pl.pallas_call(
        _ssd_kernel,
        out_shape=jax.ShapeDtypeStruct((batch, seq, SSM_D_INNER), BF16),
        grid=(seq // n,),
        in_specs=[block(SSM_CONV_DIM), block(SCAN_SCALARS * LANES), _resident((1, SSM_D_INNER))],
        out_specs=block(SSM_D_INNER),
        scratch_shapes=[pltpu.VMEM((batch, SSM_HEADS // 2, SSM_STATE, LANES), F32)],
        compiler_params=_params(1),
        name="ssd",
    )(xc.reshape(batch, seq, -1), scalars.reshape(batch, seq, -1),
      jnp.repeat(d_skip, SSM_HEAD_DIM).reshape(1, -1)).reshape(batch * seq, SSM_D_INNER)


def _merge_kernel(x_ref, ysb_ref, y_ref, z_ref, gates_ref, nrm_ref, wsb_ref, wssm_ref,
                  wo_ref, g2_ref, wu_ref, wd_ref, *rest, final_norm):
    if final_norm:
        fg_ref, o_ref, yn_ref = rest
    else:
        fg_ref, (o_ref, yn_ref) = None, rest
    tm = MERGE_TOKEN_TILE
    low = lax.broadcasted_iota(jnp.int32, (tm, LANES), 1) < SSM_HEAD_DIM
    rsum = lambda v: jnp.sum(v, axis=-1, keepdims=True)
    y_sb = _dot(ysb_ref[...], wsb_ref[...])
    y_ssm = None
    pairs_per_part = SSM_GROUPS // 4
    for gp in range(SSM_GROUPS // 2):
        tiles = [slice((3 * gp + i) * LANES, (3 * gp + i + 1) * LANES) for i in range(3)]
        ys = [y_ref[:, s].astype(F32) * _silu(z_ref[:, s].astype(F32)) for s in tiles]
        sq = [v * v for v in ys]
        ss_a = rsum(sq[0]) + rsum(jnp.where(low, sq[1], 0.0))
        ss_b = rsum(sq[2]) + rsum(jnp.where(low, 0.0, sq[1]))
        r_a = lax.rsqrt(ss_a * (1.0 / SSM_GROUP_WIDTH) + NORM_EPS)
        r_b = lax.rsqrt(ss_b * (1.0 / SSM_GROUP_WIDTH) + NORM_EPS)
        scales = [r_a, jnp.where(low, r_a, r_b), r_b]
        for s, v, r in zip(tiles, ys, scales):
            yn_ref[:, s] = (v * r * nrm_ref[:, s]).astype(BF16)
        if (gp + 1) % pairs_per_part == 0:
            part = slice((gp + 1 - pairs_per_part) * 3 * LANES, (gp + 1) * 3 * LANES)
            term = _dot(yn_ref[:, part], wssm_ref[part, :])
            y_ssm = term if y_ssm is None else y_ssm + term
    merged = (jax.nn.sigmoid(gates_ref[:, :D_MODEL].astype(F32)) * y_sb
              + jax.nn.sigmoid(gates_ref[:, D_MODEL:].astype(F32)) * y_ssm)
    mixed = x_ref[...] + _dot(merged.astype(BF16), wo_ref[...])
    o_ref[...] = _ffn_math(mixed, g2_ref, wu_ref, wd_ref, fg_ref)


def _merge(x, y_sb, y_scan, z, gates, ssm_norm, w_sb, w_ssm, w_out, ffn_gain, w_up, w_down,
           layer, final_gain):
    t, d = x.shape
    tm = MERGE_TOKEN_TILE
    row = lambda w: pl.BlockSpec((tm, w), lambda i: (i, 0))
    in_specs = [row(d), row(SB_WIDTH), row(SSM_D_INNER), row(SSM_D_INNER), row(2 * d),
                _resident((1, SSM_D_INNER)), _layer_resident(w_sb, layer),
                _layer_resident(w_ssm, layer), _layer_resident(w_out, layer),
                _resident((1, d)), _layer_resident(w_up, layer), _layer_resident(w_down, layer)]
    args = [x, y_sb, y_scan, z, gates, ssm_norm.reshape(1, -1), w_sb, w_ssm, w_out,
            ffn_gain.reshape(1, d), w_up, w_down]
    if final_gain is not None:
        in_specs.append(_resident((1, d)))
        args.append(final_gain.reshape(1, d))
    return pl.pallas_call(
        functools.partial(_merge_kernel, final_norm=final_gain is not None),
        out_shape=jax.ShapeDtypeStruct((t, d), F32),
        grid=(t // tm,),
        in_specs=in_specs,
        out_specs=row(d),
        scratch_shapes=[pltpu.VMEM((tm, SSM_D_INNER), BF16)],
        compiler_params=_params(1),
        name="merge_ffn",
    )(*args)


def kernel(x, ffn1_norm, ffn1_w_up, ffn1_w_down, mix_norm, w_in, conv_w, conv_b, dt_bias,
           a_log, d_skip, ssm_norm, w_branch_sb, w_branch_ssm, w_out, ffn2_norm, ffn2_w_up,
           ffn2_w_down, final_norm):
    batch, seq, d = x.shape
    depth = w_in.shape[0]
    assert d == D_MODEL and seq % (SB_TILE * SB_QTILES_PER_STEP) == 0
    assert seq % SSM_CHUNK == 0
    assert seq % PROJ_TOKEN_TILE == 0 and (batch * seq) % FFN_TOKEN_TILE == 0
    bf16 = lambda w: w.astype(BF16)
    up1, down1, up2, down2 = map(bf16, (ffn1_w_up, ffn1_w_down, ffn2_w_up, ffn2_w_down))
    w_sb, w_ssm, w_o, w_all = map(bf16, (w_branch_sb, w_branch_ssm, w_out, w_in))
    dt_off = 3 * SB_WIDTH + SSM_D_INNER + SSM_CONV_DIM
    w_dt = jnp.pad(w_all[:, :, dt_off:dt_off + SSM_HEADS],
                   ((0, 0), (0, 0), (0, LANES - SSM_HEADS)))
    w_gates = w_all[:, :, dt_off + SSM_HEADS:]
    xf = x.reshape(batch * seq, d)
    for l in range(depth):
        xf = _ffn(xf, ffn1_norm[l], up1, down1, l, None)
        qkv, z, xc, scalars, gates = _in_proj(
            xf, mix_norm[l], w_all, w_dt, w_gates, l, conv_w[l], conv_b[l], dt_bias[l],
            a_log[l], seq)
        y_sb = _sb_attention(qkv, batch, seq)
        y_scan = _ssd(xc, scalars, d_skip[l], batch, seq)
        xf = _merge(xf, y_sb, y_scan, z, gates, ssm_norm[l], w_sb, w_ssm, w_o, ffn2_norm[l],
                    up2, down2, l, final_norm if l == depth - 1 else None)
    return xf.reshape(batch, seq, d)
```
